```python
import math
import jax, jax.numpy as jnp
from jax import lax
import numpy as np

D_MODEL = 1024
BATCH = 32
SEQ = 256
DEPTH = 1
DEC_BATCH = 8
DEC_SEQ = 4096
PAST_LEN = 512

GRID_W = 64
N_DIR = 2
D_LRU = 1024
LRU_HEADS = 16
LRU_BLOCK = D_LRU // LRU_HEADS
CONV_W = 4
LRU_C = 8.0
D_S5 = 512
S5_GROUP = 16
S5_GROUPS = D_S5 // S5_GROUP
S5_STATE = 64
D_FF = 2816
EPS = 1e-6

kernel_name = "hybrid_rglru_s5_diffusion_step"


def _rmsnorm(x, g):
    x32 = x.astype(jnp.float32)
    y = x32 * lax.rsqrt(jnp.mean(x32 * x32, axis=-1, keepdims=True) + EPS) * g.astype(jnp.float32)
    return y.astype(x.dtype)


def _dwconv(x, w, b):
    y = lax.conv_general_dilated(x, w[:, None, :].astype(x.dtype), window_strides=(1,),
                                 padding=[(2, 1)], dimension_numbers=("NWC", "WIO", "NWC"),
                                 feature_group_count=x.shape[-1])
    return y + b.astype(x.dtype)


def _to_col_major(x):
    bsz, t, ch = x.shape
    rows = t // GRID_W
    return x.reshape(bsz, rows, GRID_W, ch).transpose(0, 2, 1, 3).reshape(bsz, t, ch)


def _from_col_major(x):
    bsz, t, ch = x.shape
    rows = t // GRID_W
    return x.reshape(bsz, GRID_W, rows, ch).transpose(0, 2, 1, 3).reshape(bsz, t, ch)


def _linear_scan(a, b, h0):
    def comb(e1, e2):
        return (e1[0] * e2[0], e2[0] * e1[1] + e2[1])
    a_cum, b_cum = lax.associative_scan(comb, (a, b), axis=1)
    return a_cum * h0[:, None] + b_cum


def _complex_linear_scan(a_re, a_im, b_re, b_im, h0_re, h0_im):
    def comb(e1, e2):
        a1r, a1i, b1r, b1i = e1
        a2r, a2i, b2r, b2i = e2
        return (a2r * a1r - a2i * a1i, a2r * a1i + a2i * a1r,
                a2r * b1r - a2i * b1i + b2r, a2r * b1i + a2i * b1r + b2i)
    ar, ai, br, bi = lax.associative_scan(comb, (a_re, a_im, b_re, b_im), axis=1)
    h0r, h0i = h0_re[:, None], h0_im[:, None]
    return ar * h0r - ai * h0i + br, ar * h0i + ai * h0r + bi


def _rglru_bidir(u, lam, w_r, b_r, w_i, b_i, h0, return_state):
    bsz, t, ch = u.shape
    out = jnp.zeros_like(u)
    finals = []
    for d in range(N_DIR):
        ud = u if d == 0 else jnp.flip(u, axis=1)
        ub = ud.reshape(bsz, t, LRU_HEADS, LRU_BLOCK)
        r = jax.nn.sigmoid(jnp.einsum("bthi,hij->bthj", ub, w_r[d].astype(jnp.float32)).reshape(bsz, t, ch)
                           + b_r[d].astype(jnp.float32))
        i = jax.nn.sigmoid(jnp.einsum("bthi,hij->bthj", ub, w_i[d].astype(jnp.float32)).reshape(bsz, t, ch)
                           + b_i[d].astype(jnp.float32))
        log_a = -LRU_C * r * jax.nn.softplus(-lam[d].astype(jnp.float32))
        a = jnp.exp(log_a)
        b = jnp.sqrt(-jnp.expm1(2.0 * log_a)) * (i * ud)
        h = _linear_scan(a, b, h0[:, d])
        if return_state:
            finals.append(h[:, -1])
        out = out + (h if d == 0 else jnp.flip(h, axis=1))
    fin = jnp.stack(finals, axis=1) if return_state else None
    return out, fin


def _s5_bidir(u, a_re, a_im, log_dt, b_re, b_im, c_re, c_im, h0_re, h0_im, return_state):
    bsz, t, ch = u.shape
    y = jnp.zeros((bsz, t, S5_GROUPS, S5_GROUP), jnp.float32)
    fin_re, fin_im = [], []
    for d in range(N_DIR):
        lr = a_re[d].astype(jnp.float32)
        li = a_im[d].astype(jnp.float32)
        dt = jnp.exp(log_dt[d].astype(jnp.float32))[:, None]
        mag = jnp.exp(lr * dt)
        abar_re, abar_im = mag * jnp.cos(li * dt), mag * jnp.sin(li * dt)
        den = lr * lr + li * li
        nr, ni = abar_re - 1.0, abar_im
        f_re = (nr * lr + ni * li) / den
        f_im = (ni * lr - nr * li) / den
        br_d, bi_d = b_re[d].astype(jnp.float32), b_im[d].astype(jnp.float32)
        bb_re = f_re[..., None] * br_d - f_im[..., None] * bi_d
        bb_im = f_re[..., None] * bi_d + f_im[..., None] * br_d
        ud = (u if d == 0 else jnp.flip(u, axis=1)).reshape(bsz, t, S5_GROUPS, S5_GROUP)
        bu_re = jnp.einsum("btgh,gph->btgp", ud, bb_re)
        bu_im = jnp.einsum("btgh,gph->btgp", ud, bb_im)
        ar = jnp.broadcast_to(abar_re, bu_re.shape)
        ai = jnp.broadcast_to(abar_im, bu_re.shape)
        h_re, h_im = _complex_linear_scan(ar, ai, bu_re, bu_im,
                                          h0_re[:, d].astype(jnp.float32), h0_im[:, d].astype(jnp.float32))
        if return_state:
            fin_re.append(h_re[:, -1])
            fin_im.append(h_im[:, -1])
        yd = (jnp.einsum("btgp,ghp->btgh", h_re, c_re[d].astype(jnp.float32))
              - jnp.einsum("btgp,ghp->btgh", h_im, c_im[d].astype(jnp.float32)))
        y = y + (yd if d == 0 else jnp.flip(yd, axis=1))
    if return_state:
        return y.reshape(bsz, t, ch), jnp.stack(fin_re, axis=1), jnp.stack(fin_im, axis=1)
    return y.reshape(bsz, t, ch), None, None


def _layer(x, cvec, p, h0_lru, h0_re, h0_im, grid_order, return_state):
    dtype = x.dtype
    mod = (jax.nn.silu(cvec) @ p["w_mod"] + p["b_mod"])[:, None, :]
    sh1, sc1, g1, sh2, sc2, g2 = jnp.split(mod, 6, axis=-1)
    hn = _rmsnorm(x, p["g_pre_mix"]) * (1.0 + sc1) + sh1
    z = hn @ p["w_in"]
    xa, ga, xs = jnp.split(z, [D_LRU, 2 * D_LRU], axis=-1)
    ua = _dwconv(xa, p["conv_w"], p["conv_b"]).astype(jnp.float32)
    ha, fin_lru = _rglru_bidir(ua, p["lru_lambda"], p["lru_w_r"], p["lru_b_r"], p["lru_w_i"], p["lru_b_i"],
                               h0_lru, return_state)
    ya = jax.nn.gelu(ga) * ha.astype(dtype)
    if grid_order:
        xs = _to_col_major(xs)
    us = xs.astype(jnp.float32)
    hs, fin_re, fin_im = _s5_bidir(us, p["s5_a_re"], p["s5_a_im"], p["s5_log_dt"], p["s5_b_re"], p["s5_b_im"],
                                   p["s5_c_re"], p["s5_c_im"], h0_re, h0_im, return_state)
    vs = jax.nn.gelu(hs + p["s5_d"].astype(jnp.float32) * us).astype(dtype)
    ys = vs * jax.nn.sigmoid(vs @ p["s5_w_glu"] + p["s5_b_glu"])
    if grid_order:
        ys = _from_col_major(ys)
    gate_a, gate_b = jnp.split(jax.nn.sigmoid(hn @ p["w_gate"] + p["b_gate"]), 2, axis=-1)
    m = (gate_a * (ya @ p["w_proj_lru"]) + gate_b * (ys @ p["w_proj_s5"])) @ p["w_out"]
    x = x + g1 * _rmsnorm(m, p["g_post_mix"])
    hn2 = _rmsnorm(x, p["g_pre_ffn"]) * (1.0 + sc2) + sh2
    u1, u3 = jnp.split(hn2 @ p["w_ff_in"], 2, axis=-1)
    f = (jax.nn.silu(u1) * u3) @ p["w_ff_out"]
    x = x + g2 * _rmsnorm(f, p["g_post_ffn"])
    return x, fin_lru, fin_re, fin_im


def setup_inputs(seed: int = 0) -> dict:
    key = jax.random.key(seed)
    ks = iter(jax.random.split(key, 48))

    def nrm(shape, scale):
        return jax.random.normal(next(ks), shape, jnp.float32) * scale

    D = D_MODEL
    a_init = jax.random.uniform(next(ks), (DEPTH, N_DIR, D_LRU), jnp.float32, 0.9, 0.999)
    n_idx = jnp.arange(S5_STATE, dtype=jnp.float32)
    return {
        "x_prompt": nrm((BATCH, SEQ, D), 1.0),
        "x_sample": nrm((DEC_BATCH, DEC_SEQ, D), 1.0),
        "c": nrm((DEC_BATCH, D), 1.0),
        "state_lru": nrm((DEC_BATCH, DEPTH, N_DIR, D_LRU), 0.5),
        "state_s5_re": nrm((DEC_BATCH, DEPTH, N_DIR, S5_GROUPS, S5_STATE), 0.5),
        "state_s5_im": nrm((DEC_BATCH, DEPTH, N_DIR, S5_GROUPS, S5_STATE), 0.5),
        "c_ctx": nrm((D,), 1.0),
        "w_mod": nrm((DEPTH, D, 6 * D), D ** -0.5),
        "b_mod": nrm((DEPTH, 6 * D), 0.02),
        "g_pre_mix": 1.0 + nrm((DEPTH, D), 0.05),
        "g_post_mix": 1.0 + nrm((DEPTH, D), 0.05),
        "g_pre_ffn": 1.0 + nrm((DEPTH, D), 0.05),
        "g_post_ffn": 1.0 + nrm((DEPTH, D), 0.05),
        "w_in": nrm((DEPTH, D, 2 * D_LRU + D_S5), D ** -0.5),
        "conv_w": nrm((DEPTH, CONV_W, D_LRU), CONV_W ** -0.5),
        "conv_b": nrm((DEPTH, D_LRU), 0.02),
        "lru_w_r": nrm((DEPTH, N_DIR, LRU_HEADS, LRU_BLOCK, LRU_BLOCK), LRU_BLOCK ** -0.5),
        "lru_b_r": nrm((DEPTH, N_DIR, D_LRU), 0.02),
        "lru_w_i": nrm((DEPTH, N_DIR, LRU_HEADS, LRU_BLOCK, LRU_BLOCK), LRU_BLOCK ** -0.5),
        "lru_b_i": nrm((DEPTH, N_DIR, D_LRU), 0.02),
        "lru_lambda": jnp.log(a_init / (1.0 - a_init)),
        "s5_a_re": -0.5 + nrm((DEPTH, N_DIR, S5_GROUPS, S5_STATE), 0.01),
        "s5_a_im": math.pi * n_idx + nrm((DEPTH, N_DIR, S5_GROUPS, S5_STATE), 0.01),
        "s5_log_dt": jax.random.uniform(next(ks), (DEPTH, N_DIR, S5_GROUPS), jnp.float32,
                                        math.log(1e-3), math.log(1e-1)),
        "s5_b_re": nrm((DEPTH, N_DIR, S5_GROUPS, S5_STATE, S5_GROUP), (2 * S5_GROUP) ** -0.5),
        "s5_b_im": nrm((DEPTH, N_DIR, S5_GROUPS, S5_STATE, S5_GROUP), (2 * S5_GROUP) ** -0.5),
        "s5_c_re": nrm((DEPTH, N_DIR, S5_GROUPS, S5_GROUP, S5_STATE), S5_STATE ** -0.5),
        "s5_c_im": nrm((DEPTH, N_DIR, S5_GROUPS, S5_GROUP, S5_STATE), S5_STATE ** -0.5),
        "s5_d": nrm((DEPTH, D_S5), 0.5),
        "s5_w_glu": nrm((DEPTH, D_S5, D_S5), D_S5 ** -0.5),
        "s5_b_glu": nrm((DEPTH, D_S5), 0.02),
        "w_proj_lru": nrm((DEPTH, D_LRU, D), D_LRU ** -0.5),
        "w_proj_s5": nrm((DEPTH, D_S5, D), D_S5 ** -0.5),
        "w_gate": nrm((DEPTH, D, 2 * D), D ** -0.5),
        "b_gate": nrm((DEPTH, 2 * D), 0.02),
        "w_out": nrm((DEPTH, D, D), D ** -0.5),
        "w_ff_in": nrm((DEPTH, D, 2 * D_FF), D ** -0.5),
        "w_ff_out": nrm((DEPTH, D_FF, D), D_FF ** -0.5),
    }


def reference(x_prompt, x_sample, c, state_lru, state_s5_re, state_s5_im, c_ctx,
              w_mod, b_mod, g_pre_mix, g_post_mix, g_pre_ffn, g_post_ffn, w_in, conv_w, conv_b,
              lru_w_r, lru_b_r, lru_w_i, lru_b_i, lru_lambda,
              s5_a_re, s5_a_im, s5_log_dt, s5_b_re, s5_b_im, s5_c_re, s5_c_im, s5_d, s5_w_glu, s5_b_glu,
              w_proj_lru, w_proj_s5, w_gate, b_gate, w_out, w_ff_in, w_ff_out):
    stacked = (("w_mod", w_mod), ("b_mod", b_mod), ("g_pre_mix", g_pre_mix), ("g_post_mix", g_post_mix),
               ("g_pre_ffn", g_pre_ffn), ("g_post_ffn", g_post_ffn), ("w_in", w_in), ("conv_w", conv_w),
               ("conv_b", conv_b), ("lru_w_r", lru_w_r), ("lru_b_r", lru_b_r), ("lru_w_i", lru_w_i),
               ("lru_b_i", lru_b_i), ("lru_lambda", lru_lambda), ("s5_a_re", s5_a_re), ("s5_a_im", s5_a_im),
               ("s5_log_dt", s5_log_dt), ("s5_b_re", s5_b_re), ("s5_b_im", s5_b_im), ("s5_c_re", s5_c_re),
               ("s5_c_im", s5_c_im), ("s5_d", s5_d), ("s5_w_glu", s5_w_glu), ("s5_b_glu", s5_b_glu),
               ("w_proj_lru", w_proj_lru), ("w_proj_s5", w_proj_s5), ("w_gate", w_gate), ("b_gate", b_gate),
               ("w_out", w_out), ("w_ff_in", w_ff_in), ("w_ff_out", w_ff_out))
    n_ctx = x_prompt.shape[0]
    ctx_c = jnp.broadcast_to(c_ctx, (n_ctx, D_MODEL))
    zero_lru = jnp.zeros((n_ctx, N_DIR, D_LRU), jnp.float32)
    zero_s5 = jnp.zeros((n_ctx, N_DIR, S5_GROUPS, S5_STATE), jnp.float32)
    y_prompt, y_sample = x_prompt, x_sample
    lru_list, re_list, im_list = [], [], []
    for l in range(DEPTH):
        p = {name: arr[l] for name, arr in stacked}
        y_prompt, f_lru, f_re, f_im = _layer(y_prompt, ctx_c, p, zero_lru, zero_s5, zero_s5,
                                             grid_order=False, return_state=True)
        lru_list.append(f_lru)
        re_list.append(f_re)
        im_list.append(f_im)
        y_sample, _, _, _ = _layer(y_sample, c, p, state_lru[:, l].astype(jnp.float32),
                                   state_s5_re[:, l], state_s5_im[:, l],
                                   grid_order=True, return_state=False)
    new_state_lru = jnp.stack(lru_list, axis=1)
    new_state_s5_re = jnp.stack(re_list, axis=1)
    new_state_s5_im = jnp.stack(im_list, axis=1)
    return (y_prompt, y_sample, new_state_lru, new_state_s5_re, new_state_s5_im)
```

```python
import functools

import jax
import jax.numpy as jnp
from jax import lax
from jax.experimental import pallas as pl
from jax.experimental.pallas import tpu as pltpu

F32 = jnp.float32
BF16 = jnp.bfloat16

EPS = 1e-6
LRU_C = 8.0
GRID_W = 64
SUBLANES = 8
MXU_TILE = 256
S5_GROUP = 16
S5_STATE = 64
VMEM_LIMIT = 56 * 1024 * 1024

T_BLK = 64


def _const_spec(shape):
    nd = len(shape)
    return pl.BlockSpec(shape, lambda *_: (0,) * nd, pipeline_mode=pl.Buffered(1))


def _params(sem):
    return pltpu.CompilerParams(dimension_semantics=sem, vmem_limit_bytes=VMEM_LIMIT)


def _dot(a, b):
    return jnp.dot(a, b, preferred_element_type=F32)


def _rms(x):
    return x * lax.rsqrt(jnp.mean(x * x, axis=-1, keepdims=True) + EPS)


def _mod_kernel(c_ref, w_ref, b_ref, o_ref):
    c = c_ref[...]
    s = c * jax.nn.sigmoid(c)
    o_ref[...] = jnp.dot(s, w_ref[...], preferred_element_type=F32,
                         precision=lax.Precision.HIGHEST) + b_ref[...]


def _mod_call(cvecs, w_mod, b_mod):
    rows, d = cvecs.shape
    n = w_mod.shape[1]
    nb = 4
    return pl.pallas_call(
        _mod_kernel,
        grid=(nb,),
        in_specs=[pl.BlockSpec((rows, d), lambda j: (0, 0)),
                  pl.BlockSpec((d, n // nb), lambda j: (0, j)),
                  pl.BlockSpec((1, n // nb), lambda j: (0, j))],
        out_specs=pl.BlockSpec((rows, n // nb), lambda j: (0, j)),
        out_shape=jax.ShapeDtypeStruct((rows, n), F32),
        compiler_params=_params(("arbitrary",)),
        name="mod",
    )(cvecs, w_mod, b_mod.reshape(1, n))


def _stage1_kernel(x_ref, mod_ref, g_ref, w_ref, xtm_ref, xa_ref, ga_ref, xs_ref, *, colmajor):
    _, tt, d = x_ref.shape
    rows = tt * SUBLANES
    x = jnp.swapaxes(x_ref[...], 0, 1)
    sh1 = mod_ref[:, 0:d][None]
    sc1 = mod_ref[:, d:2 * d][None]
    hn = _rms(x) * g_ref[...] * (1.0 + sc1) + sh1
    xtm_ref[...] = x.reshape(rows, d)
    z = _dot(hn.reshape(rows, d).astype(BF16), w_ref[...])
    dl = xa_ref.shape[-1]
    xa_ref[...] = z[:, 0:dl].astype(BF16)
    ga_ref[...] = z[:, dl:2 * dl].astype(BF16)
    zs = z[:, 2 * dl:]
    if colmajor:
        xs_ref[:, 0] = zs.reshape(tt, SUBLANES, zs.shape[-1])
    else:
        xs_ref[...] = zs.reshape(tt, SUBLANES, zs.shape[-1])


def _stage1_call(x, mod8, g_pre, w_in, d_lru, d_s5, colmajor):
    b, t, d = x.shape
    g = b // SUBLANES
    tt = T_BLK
    nt = t // tt
    rows = tt * SUBLANES
    if colmajor:
        assert g == 1 and tt == GRID_W
        xs_shape = (GRID_W, nt, SUBLANES, d_s5)
        xs_spec = pl.BlockSpec((GRID_W, 1, SUBLANES, d_s5), lambda gi, ti: (0, ti, 0, 0))
    else:
        xs_shape = (g, t, SUBLANES, d_s5)
        xs_spec = pl.BlockSpec((None, tt, SUBLANES, d_s5), lambda gi, ti: (gi, ti, 0, 0))
    row_spec = lambda c: pl.BlockSpec((None, rows, c), lambda gi, ti: (gi, ti, 0))
    return pl.pallas_call(
        functools.partial(_stage1_kernel, colmajor=colmajor),
        grid=(g, nt),
        in_specs=[pl.BlockSpec((SUBLANES, tt, d), lambda gi, ti: (gi, ti, 0)),
                  _const_spec(mod8.shape), _const_spec((1, d)), _const_spec(w_in.shape)],
        out_specs=[row_spec(d), row_spec(d_lru), row_spec(d_lru), xs_spec],
        out_shape=[jax.ShapeDtypeStruct((g, t * SUBLANES, d), F32),
                   jax.ShapeDtypeStruct((g, t * SUBLANES, d_lru), BF16),
                   jax.ShapeDtypeStruct((g, t * SUBLANES, d_lru), BF16),
                   jax.ShapeDtypeStruct(xs_shape, F32)],
        compiler_params=_params(("arbitrary", "arbitrary")),
        name="stage1",
    )(x, mod8, g_pre.reshape(1, d), w_in)


def _lru_kernel(xm_ref, xp_ref, xn_ref, cw_ref, cb_ref, wr_ref, wi_ref, br_ref, bi_ref, sp_ref, h0_ref,
                h_ref, fin_ref, xe_scr, a_scr, b_scr, carry, *, reverse, nt):
    rows, c = xm_ref.shape
    tt = rows // SUBLANES
    i = pl.program_id(1)
    blk = (nt - 1 - i) if reverse else i

    @pl.when(i == 0)
    def _():
        carry[...] = h0_ref[...]

    halo = 2 * SUBLANES
    prev_ok = (blk > 0).astype(F32)
    next_ok = (blk < nt - 1).astype(F32)
    xe_scr[0:halo] = xp_ref[...].astype(F32) * prev_ok
    xe_scr[halo:halo + rows] = xm_ref[...].astype(F32)
    xe_scr[halo + rows:halo + rows + SUBLANES] = xn_ref[0:SUBLANES].astype(F32) * next_ok

    sp = sp_ref[...]
    for j in range(c // MXU_TILE):
        cs = slice(j * MXU_TILE, (j + 1) * MXU_TILE)
        u = cb_ref[:, cs]
        for k in range(4):
            u = u + cw_ref[k:k + 1, cs] * xe_scr[k * SUBLANES:k * SUBLANES + rows, cs]
        ub = u.astype(BF16)
        r = jax.nn.sigmoid(_dot(ub, wr_ref[j]) + br_ref[:, cs])
        ig = jax.nn.sigmoid(_dot(ub, wi_ref[j]) + bi_ref[:, cs])
        a = jnp.exp((-LRU_C) * r * sp[:, cs])
        a_scr[:, cs] = a
        b_scr[:, cs] = jnp.sqrt(1.0 - a * a) * (ig * u)

    def step(s, h):
        t = (tt - 1 - s) if reverse else s
        sl = pl.ds(pl.multiple_of(t * SUBLANES, SUBLANES), SUBLANES)
        h = a_scr[sl, :] * h + b_scr[sl, :]
        b_scr[sl, :] = h
        return h

    h = lax.fori_loop(0, tt, step, carry[...], unroll=8)
    carry[...] = h
    fin_ref[...] = h
    h_ref[...] = b_scr[...].astype(BF16)


def _lru_call(xa, conv_w, conv_b, wr_bd, wi_bd, b_r, b_i, sp, h0, reverse):
    g, n, c = xa.shape
    tt = T_BLK
    rows = tt * SUBLANES
    nt = n // rows
    hb = 2 * SUBLANES
    per = rows // hb
    nhb = n // hb
    blk = (lambda ti: nt - 1 - ti) if reverse else (lambda ti: ti)
    main = pl.BlockSpec((None, rows, c), lambda gi, ti: (gi, blk(ti), 0))
    prev = pl.BlockSpec((None, hb, c), lambda gi, ti: (gi, jnp.maximum(blk(ti) * per - 1, 0), 0))
    nxt = pl.BlockSpec((None, hb, c), lambda gi, ti: (gi, jnp.minimum((blk(ti) + 1) * per, nhb - 1), 0))
    vec = lambda: _const_spec((1, c))
    return pl.pallas_call(
        functools.partial(_lru_kernel, reverse=reverse, nt=nt),
        grid=(g, nt),
        in_specs=[main, prev, nxt, _const_spec((4, c)), vec(), _const_spec(wr_bd.shape), _const_spec(wi_bd.shape),
                  vec(), vec(), vec(), pl.BlockSpec((None, SUBLANES, c), lambda gi, ti: (gi, 0, 0))],
        out_specs=[main, pl.BlockSpec((None, SUBLANES, c), lambda gi, ti: (gi, 0, 0))],
        out_shape=[jax.ShapeDtypeStruct((g, n, c), BF16), jax.ShapeDtypeStruct((g, SUBLANES, c), F32)],
        scratch_shapes=[pltpu.VMEM((rows + 4 * SUBLANES, c), F32), pltpu.VMEM((rows, c), F32),
                        pltpu.VMEM((rows, c), F32), pltpu.VMEM((SUBLANES, c), F32)],
        compiler_params=_params(("arbitrary", "arbitrary")),
        name="lru_bwd" if reverse else "lru_fwd",
    )(xa, xa, xa, conv_w, conv_b.reshape(1, c), wr_bd, wi_bd, b_r.reshape(1, c), b_i.reshape(1, c),
      sp.reshape(1, c), h0)


S5_LANE_CHUNK = 512


def _s5_kernel(xs_ref, bre_ref, bim_ref, are_ref, aim_ref, cre_ref, cim_ref, h0r_ref, h0i_ref,
               y_ref, finr_ref, fini_ref, ur_scr, ui_scr, cr, ci, *, reverse):
    tt = xs_ref.shape[0]
    ch = xs_ref.shape[-1]
    rows = tt * SUBLANES
    ns = ur_scr.shape[-1]
    i = pl.program_id(1)

    @pl.when(i == 0)
    def _():
        cr[...] = h0r_ref[...]
        ci[...] = h0i_ref[...]

    x = xs_ref[...].reshape(rows, ch).astype(BF16)
    ntile = ch // MXU_TILE
    per = ns // ntile
    for j in range(ntile):
        xj = x[:, j * MXU_TILE:(j + 1) * MXU_TILE]
        ur_scr[:, j * per:(j + 1) * per] = _dot(xj, bre_ref[j])
        ui_scr[:, j * per:(j + 1) * per] = _dot(xj, bim_ref[j])

    for q in range(ns // S5_LANE_CHUNK):
        ls = slice(q * S5_LANE_CHUNK, (q + 1) * S5_LANE_CHUNK)
        ar = jnp.broadcast_to(are_ref[:, ls], (SUBLANES, S5_LANE_CHUNK))
        ai = jnp.broadcast_to(aim_ref[:, ls], (SUBLANES, S5_LANE_CHUNK))

        def step(s, carry_, ls=ls, ar=ar, ai=ai):
            hr, hi = carry_
            t = (tt - 1 - s) if reverse else s
            sl = pl.ds(pl.multiple_of(t * SUBLANES, SUBLANES), SUBLANES)
            nr = ar * hr - ai * hi + ur_scr[sl, ls]
            ni = ar * hi + ai * hr + ui_scr[sl, ls]
            ur_scr[sl, ls] = nr
            ui_scr[sl, ls] = ni
            return nr, ni

        hr, hi = lax.fori_loop(0, tt, step, (cr[:, ls], ci[:, ls]), unroll=4)
        cr[:, ls] = hr
        ci[:, ls] = hi

    finr_ref[...] = cr[...]
    fini_ref[...] = ci[...]
    ys = []
    for j in range(ntile):
        hrj = ur_scr[:, j * per:(j + 1) * per].astype(BF16)
        hij = ui_scr[:, j * per:(j + 1) * per].astype(BF16)
        ys.append(_dot(hrj, cre_ref[j]) - _dot(hij, cim_ref[j]))
    y_ref[...] = jnp.concatenate(ys, axis=1).reshape(tt, SUBLANES, ch)


def _s5_call(xs, b_re, b_im, a_re, a_im, c_re, c_im, h0_re, h0_im, reverse):
    g, t, _, ch = xs.shape
    ns = a_re.shape[-1]
    tt = T_BLK
    nt = t // tt
    blk = (lambda ti: nt - 1 - ti) if reverse else (lambda ti: ti)
    main = pl.BlockSpec((None, tt, SUBLANES, ch), lambda gi, ti: (gi, blk(ti), 0, 0))
    st = pl.BlockSpec((None, SUBLANES, ns), lambda gi, ti: (gi, 0, 0))
    rows = tt * SUBLANES
    return pl.pallas_call(
        functools.partial(_s5_kernel, reverse=reverse),
        grid=(g, nt),
        in_specs=[main, _const_spec(b_re.shape), _const_spec(b_im.shape), _const_spec((1, ns)),
                  _const_spec((1, ns)), _const_spec(c_re.shape), _const_spec(c_im.shape), st, st],
        out_specs=[main, st, st],
        out_shape=[jax.ShapeDtypeStruct(xs.shape, F32), jax.ShapeDtypeStruct((g, SUBLANES, ns), F32),
                   jax.ShapeDtypeStruct((g, SUBLANES, ns), F32)],
        scratch_shapes=[pltpu.VMEM((rows, ns), F32), pltpu.VMEM((rows, ns), F32),
                        pltpu.VMEM((SUBLANES, ns), F32), pltpu.VMEM((SUBLANES, ns), F32)],
        compiler_params=_params(("arbitrary", "arbitrary")),
        name="s5_bwd" if reverse else "s5_fwd",
    )(xs, b_re, b_im, a_re.reshape(1, ns), a_im.reshape(1, ns), c_re, c_im, h0_re, h0_im)


def _mix_kernel(x_ref, ga_ref, hf_ref, hb_ref, yf_ref, yb_ref, xs_ref, mod_ref, gpre_ref, gpost_ref,
                wg_ref, bg_ref, wpl_ref, wps_ref, wo_ref, d_ref, wglu_ref, bglu_ref, o_ref, *, colmajor):
    rows, d = x_ref.shape
    tt = rows // SUBLANES
    x3 = x_ref[...].reshape(tt, SUBLANES, d)
    sh1 = mod_ref[:, 0:d][None]
    sc1 = mod_ref[:, d:2 * d][None]
    g1 = mod_ref[:, 2 * d:3 * d][None]
    hn = (_rms(x3) * gpre_ref[...] * (1.0 + sc1) + sh1).reshape(rows, d).astype(BF16)
    gate = jax.nn.sigmoid(_dot(hn, wg_ref[...]) + bg_ref[...])

    ha = hf_ref[...].astype(F32) + hb_ref[...].astype(F32)
    ya = jax.nn.gelu(ga_ref[...].astype(F32)) * ha
    pa = _dot(ya.astype(BF16), wpl_ref[...])

    def s5_rows(ref):
        v = ref[:, 0] if colmajor else ref[...]
        return v.reshape(rows, v.shape[-1])

    us = s5_rows(xs_ref)
    hs = s5_rows(yf_ref) + s5_rows(yb_ref)
    vs = jax.nn.gelu(hs + d_ref[...] * us)
    ys = vs * jax.nn.sigmoid(_dot(vs.astype(BF16), wglu_ref[...]) + bglu_ref[...])
    pb = _dot(ys.astype(BF16), wps_ref[...])

    mm = gate[:, 0:d] * pa + gate[:, d:2 * d] * pb
    m = _dot(mm.astype(BF16), wo_ref[...])
    mn = (_rms(m) * gpost_ref[...]).reshape(tt, SUBLANES, d)
    o_ref[...] = (x3 + g1 * mn).reshape(rows, d)


def _mix_call(xtm, ga, hf, hb, yf, yb, xs, mod8, g_pre, g_post, w_gate, b_gate, w_pl, w_ps, w_out,
              s5_d, w_glu, b_glu, colmajor):
    g, n, d = xtm.shape
    tt = T_BLK
    rows = tt * SUBLANES
    nt = n // rows
    ch = xs.shape[-1]
    row_spec = lambda c: pl.BlockSpec((None, rows, c), lambda gi, ti: (gi, ti, 0))
    if colmajor:
        s5_spec = pl.BlockSpec((GRID_W, 1, SUBLANES, ch), lambda gi, ti: (0, ti, 0, 0))
    else:
        s5_spec = pl.BlockSpec((None, tt, SUBLANES, ch), lambda gi, ti: (gi, ti, 0, 0))
    dl = ga.shape[-1]
    consts = [mod8, g_pre.reshape(1, d), g_post.reshape(1, d), w_gate, b_gate.reshape(1, 2 * d), w_pl, w_ps,
              w_out, s5_d.reshape(1, ch), w_glu, b_glu.reshape(1, ch)]
    return pl.pallas_call(
        functools.partial(_mix_kernel, colmajor=colmajor),
        grid=(g, nt),
        in_specs=[row_spec(d), row_spec(dl), row_spec(dl), row_spec(dl), s5_spec, s5_spec, s5_spec]
                 + [_const_spec(a.shape) for a in consts],
        out_specs=row_spec(d),
        out_shape=jax.ShapeDtypeStruct((g, n, d), F32),
        compiler_params=_params(("arbitrary", "arbitrary")),
        name="mix",
    )(xtm, ga, hf, hb, yf, yb, xs, *consts)


FF_CHUNKS = 2


def _ffn_kernel(x_ref, mod_ref, gpre_ref, gpost_ref, w1_ref, w3_ref, w2_ref, o_ref):
    rows, d = x_ref.shape
    tt = rows // SUBLANES
    x3 = x_ref[...].reshape(tt, SUBLANES, d)
    sh2 = mod_ref[:, 3 * d:4 * d][None]
    sc2 = mod_ref[:, 4 * d:5 * d][None]
    g2 = mod_ref[:, 5 * d:6 * d][None]
    hn = (_rms(x3) * gpre_ref[...] * (1.0 + sc2) + sh2).reshape(rows, d).astype(BF16)
    dff = w1_ref.shape[-1]
    cw = dff // FF_CHUNKS
    f = jnp.zeros((rows, d), F32)
    for k in range(FF_CHUNKS):
        cs = slice(k * cw, (k + 1) * cw)
        u1 = _dot(hn, w1_ref[:, cs])
        u3 = _dot(hn, w3_ref[:, cs])
        act = (u1 * jax.nn.sigmoid(u1) * u3).astype(BF16)
        f = f + _dot(act, w2_ref[cs, :])
    fn = (_rms(f) * gpost_ref[...]).reshape(tt, SUBLANES, d)
    o_ref[...] = jnp.swapaxes(x3 + g2 * fn, 0, 1)


def _ffn_call(x1, mod8, g_pre, g_post, w1, w3, w2):
    g, n, d = x1.shape
    tt = T_BLK
    rows = tt * SUBLANES
    nt = n // rows
    consts = [mod8, g_pre.reshape(1, d), g_post.reshape(1, d), w1, w3, w2]
    return pl.pallas_call(
        _ffn_kernel,
        grid=(g, nt),
        in_specs=[pl.BlockSpec((None, rows, d), lambda gi, ti: (gi, ti, 0))]
                 + [_const_spec(a.shape) for a in consts],
        out_specs=pl.BlockSpec((SUBLANES, tt, d), lambda gi, ti: (gi, ti, 0)),
        out_shape=jax.ShapeDtypeStruct((g * SUBLANES, nt * tt, d), F32),
        compiler_params=_params(("arbitrary", "arbitrary")),
        name="ffn",
    )(x1, *consts)


def _block_diag_tiles(w, per_tile):
    nb, k, m = w.shape
    w = w.reshape(nb // per_tile, per_tile, k, m)
    eye = jnp.eye(per_tile, dtype=w.dtype)
    t = jnp.einsum("tbkm,bc->tbkcm", w, eye)
    return t.reshape(nb // per_tile, per_tile * k, per_tile * m)


def _s5_discretize(a_re, a_im, log_dt, b_re, b_im):
    dt = jnp.exp(log_dt)[:, None]
    mag = jnp.exp(a_re * dt)
    abar_re, abar_im = mag * jnp.cos(a_im * dt), mag * jnp.sin(a_im * dt)
    den = a_re * a_re + a_im * a_im
    nr, ni = abar_re - 1.0, abar_im
    f_re = (nr * a_re + ni * a_im) / den
    f_im = (ni * a_re - nr * a_im) / den
    bb_re = f_re[..., None] * b_re - f_im[..., None] * b_im
    bb_im = f_re[..., None] * b_im + f_im[..., None] * b_re
    return abar_re, abar_im, bb_re, bb_im


def _layer(x, mod8, p, h0_lru, h0_re, h0_im, colmajor):
    b, t, d = x.shape
    d_lru = p["conv_b"].shape[-1]
    d_s5 = p["s5_d"].shape[-1]
    xtm, xa, ga, xs = _stage1_call(x, mod8, p["g_pre_mix"], p["w_in"], d_lru, d_s5, colmajor)
    if colmajor:
        xs_seq = xs.reshape(1, GRID_W * xs.shape[1], SUBLANES, d_s5)
    else:
        xs_seq = xs
    hs, fins = [], []
    ys, fr, fi = [], [], []
    for dr in range(2):
        h, fin = _lru_call(xa, p["conv_w"], p["conv_b"], p["wr_bd"][dr], p["wi_bd"][dr], p["lru_b_r"][dr],
                           p["lru_b_i"][dr], p["lru_sp"][dr], h0_lru[dr], reverse=bool(dr))
        hs.append(h)
        fins.append(fin)
        y, r_, i_ = _s5_call(xs_seq, p["s5_bre"][dr], p["s5_bim"][dr], p["s5_are"][dr], p["s5_aim"][dr],
                             p["s5_cre"][dr], p["s5_cim"][dr], h0_re[dr], h0_im[dr], reverse=bool(dr))
        ys.append(y.reshape(xs.shape))
        fr.append(r_)
        fi.append(i_)
    x1 = _mix_call(xtm, ga, hs[0], hs[1], ys[0], ys[1], xs, mod8, p["g_pre_mix"], p["g_post_mix"], p["w_gate"],
                   p["b_gate"], p["w_proj_lru"], p["w_proj_s5"], p["w_out"], p["s5_d"], p["s5_w_glu"],
                   p["s5_b_glu"], colmajor)
    out = _ffn_call(x1, mod8, p["g_pre_ffn"], p["g_post_ffn"], p["w_ff1"], p["w_ff3"], p["w_ff2"])
    return out, fins, fr, fi


def kernel(x_prompt, x_sample, c, state_lru, state_s5_re, state_s5_im, c_ctx, w_mod, b_mod, g_pre_mix, g_post_mix,
           g_pre_ffn, g_post_ffn, w_in, conv_w, conv_b, lru_w_r, lru_b_r, lru_w_i, lru_b_i, lru_lambda,
           s5_a_re, s5_a_im, s5_log_dt, s5_b_re, s5_b_im, s5_c_re, s5_c_im, s5_d, s5_w_glu, s5_b_glu,
           w_proj_lru, w_proj_s5, w_gate, b_gate, w_out, w_ff_in, w_ff_out):
    depth = w_mod.shape[0]
    n_ctx, _, d = x_prompt.shape
    n_dec = x_sample.shape[0]
    assert n_dec == SUBLANES and n_ctx % SUBLANES == 0
    g_ctx = n_ctx // SUBLANES
    n_dir, n_grp, n_st = s5_a_re.shape[1:]
    d_lru = conv_b.shape[-1]
    d_ff = w_ff_out.shape[1]
    heads_per_tile = MXU_TILE // lru_w_r.shape[-1]
    grp_per_tile = MXU_TILE // S5_GROUP

    y_prompt, y_sample = x_prompt, x_sample
    lru_list, re_list, im_list = [], [], []
    for l in range(depth):
        cvecs = jnp.concatenate([c, c_ctx[None], jnp.zeros((SUBLANES - 1, d), F32)], axis=0)
        mod = _mod_call(cvecs, w_mod[l], b_mod[l])
        mod_dec = mod[0:SUBLANES]
        mod_ctx = jnp.broadcast_to(mod[SUBLANES:SUBLANES + 1], (SUBLANES, mod.shape[1]))

        disc = [_s5_discretize(s5_a_re[l, dr], s5_a_im[l, dr], s5_log_dt[l, dr], s5_b_re[l, dr], s5_b_im[l, dr])
                for dr in range(n_dir)]
        p = {
            "g_pre_mix": g_pre_mix[l], "g_post_mix": g_post_mix[l], "g_pre_ffn": g_pre_ffn[l],
            "g_post_ffn": g_post_ffn[l], "w_in": w_in[l].astype(BF16), "conv_w": conv_w[l], "conv_b": conv_b[l],
            "wr_bd": [_block_diag_tiles(lru_w_r[l, dr], heads_per_tile).astype(BF16) for dr in range(n_dir)],
            "wi_bd": [_block_diag_tiles(lru_w_i[l, dr], heads_per_tile).astype(BF16) for dr in range(n_dir)],
            "lru_b_r": lru_b_r[l], "lru_b_i": lru_b_i[l],
            "lru_sp": jax.nn.softplus(-lru_lambda[l]),
            "s5_bre": [_block_diag_tiles(jnp.swapaxes(dd[2], 1, 2), grp_per_tile).astype(BF16) for dd in disc],
            "s5_bim": [_block_diag_tiles(jnp.swapaxes(dd[3], 1, 2), grp_per_tile).astype(BF16) for dd in disc],
            "s5_are": [dd[0].reshape(-1) for dd in disc], "s5_aim": [dd[1].reshape(-1) for dd in disc],
            "s5_cre": [_block_diag_tiles(jnp.swapaxes(s5_c_re[l, dr], 1, 2), grp_per_tile).astype(BF16)
                       for dr in range(n_dir)],
            "s5_cim": [_block_diag_tiles(jnp.swapaxes(s5_c_im[l, dr], 1, 2), grp_per_tile).astype(BF16)
                       for dr in range(n_dir)],
            "s5_d": s5_d[l], "s5_w_glu": s5_w_glu[l].astype(BF16), "s5_b_glu": s5_b_glu[l],
            "w_proj_lru": w_proj_lru[l].astype(BF16), "w_proj_s5": w_proj_s5[l].astype(BF16),
            "w_gate": w_gate[l].astype(BF16), "b_gate": b_gate[l], "w_out": w_out[l].astype(BF16),
            "w_ff1": w_ff_in[l, :, :d_ff].astype(BF16), "w_ff3": w_ff_in[l, :, d_ff:].astype(BF16),
            "w_ff2": w_ff_out[l].astype(BF16),
        }
        ns = n_grp * n_st
        zero_lru = [jnp.zeros((g_ctx, SUBLANES, d_lru), F32)] * n_dir
        zero_s5 = [jnp.zeros((g_ctx, SUBLANES, ns), F32)] * n_dir
        y_prompt, f_lru, f_re, f_im = _layer(y_prompt, mod_ctx, p, zero_lru, zero_s5, zero_s5, colmajor=False)
        lru_list.append(jnp.stack([f.reshape(n_ctx, d_lru) for f in f_lru], axis=1))
        re_list.append(jnp.stack([f.reshape(n_ctx, n_grp, n_st) for f in f_re], axis=1))
        im_list.append(jnp.stack([f.reshape(n_ctx, n_grp, n_st) for f in f_im], axis=1))

        h0_lru = [state_lru[:, l, dr].astype(F32)[None] for dr in range(n_dir)]
        h0_re = [state_s5_re[:, l, dr].reshape(1, n_dec, ns) for dr in range(n_dir)]
        h0_im = [state_s5_im[:, l, dr].reshape(1, n_dec, ns) for dr in range(n_dir)]
        y_sample, _, _, _ = _layer(y_sample, mod_dec, p, h0_lru, h0_re, h0_im, colmajor=True)
    return (y_prompt, y_sample, jnp.stack(lru_list, axis=1), jnp.stack(re_list, axis=1),
            jnp.stack(im_list, axis=1))
```

```python
import functools

import jax
import jax.numpy as jnp
from jax import lax
from jax.experimental import pallas as pl
from jax.experimental.pallas import tpu as pltpu

F32 = jnp.float32
BF16 = jnp.bfloat16

EPS = 1e-6
LRU_C = 8.0
LOG2_E = 1.4426950408889634
GRID_W = 64
SUBLANES = 8
MXU_TILE = 256
S5_GROUP = 16
S5_STATE = 64
VMEM_LIMIT = 56 * 1024 * 1024

T_BLK = 64


def _const_spec(shape):
    nd = len(shape)
    return pl.BlockSpec(shape, lambda *_: (0,) * nd, pipeline_mode=pl.Buffered(1))


def _params(sem):
    return pltpu.CompilerParams(dimension_semantics=sem, vmem_limit_bytes=VMEM_LIMIT)


def _dot(a, b):
    return jnp.dot(a, b, preferred_element_type=F32)


def _rms(x):
    return x * lax.rsqrt(jnp.mean(x * x, axis=-1, keepdims=True) + EPS)


def _mod_kernel(c_ref, w_ref, b_ref, o_ref):
    c = c_ref[...]
    s = c * jax.nn.sigmoid(c)
    o_ref[...] = jnp.dot(s, w_ref[...], preferred_element_type=F32,
                         precision=lax.Precision.HIGHEST) + b_ref[...]


def _mod_call(cvecs, w_mod, b_mod):
    rows, d = cvecs.shape
    n = w_mod.shape[1]
    nb = 4
    return pl.pallas_call(
        _mod_kernel,
        grid=(nb,),
        in_specs=[pl.BlockSpec((rows, d), lambda j: (0, 0)),
                  pl.BlockSpec((d, n // nb), lambda j: (0, j)),
                  pl.BlockSpec((1, n // nb), lambda j: (0, j))],
        out_specs=pl.BlockSpec((rows, n // nb), lambda j: (0, j)),
        out_shape=jax.ShapeDtypeStruct((rows, n), F32),
        compiler_params=_params(("arbitrary",)),
        name="mod",
    )(cvecs, w_mod, b_mod.reshape(1, n))


def _stage1_kernel(x_ref, mod_ref, g_ref, w_ref, xtm_ref, xa_ref, ga_ref, xs_ref, *, colmajor):
    _, tt, d = x_ref.shape
    rows = tt * SUBLANES
    x = jnp.swapaxes(x_ref[...], 0, 1)
    sh1 = mod_ref[:, 0:d][None]
    sc1 = mod_ref[:, d:2 * d][None]
    hn = _rms(x) * g_ref[...] * (1.0 + sc1) + sh1
    xtm_ref[...] = x.reshape(rows, d)
    z = _dot(hn.reshape(rows, d).astype(BF16), w_ref[...])
    dl = xa_ref.shape[-1]
    xa_ref[...] = z[:, 0:dl].astype(BF16)
    ga_ref[...] = z[:, dl:2 * dl].astype(BF16)
    zs = z[:, 2 * dl:]
    if colmajor:
        xs_ref[:, 0] = zs.reshape(tt, SUBLANES, zs.shape[-1])
    else:
        xs_ref[...] = zs.reshape(tt, SUBLANES, zs.shape[-1])


def _stage1_call(x, mod8, g_pre, w_in, d_lru, d_s5, colmajor):
    b, t, d = x.shape
    g = b // SUBLANES
    tt = T_BLK
    nt = t // tt
    rows = tt * SUBLANES
    if colmajor:
        assert g == 1 and tt == GRID_W
        xs_shape = (GRID_W, nt, SUBLANES, d_s5)
        xs_spec = pl.BlockSpec((GRID_W, 1, SUBLANES, d_s5), lambda gi, ti: (0, ti, 0, 0))
    else:
        xs_shape = (g, t, SUBLANES, d_s5)
        xs_spec = pl.BlockSpec((None, tt, SUBLANES, d_s5), lambda gi, ti: (gi, ti, 0, 0))
    row_spec = lambda c: pl.BlockSpec((None, rows, c), lambda gi, ti: (gi, ti, 0))
    return pl.pallas_call(
        functools.partial(_stage1_kernel, colmajor=colmajor),
        grid=(g, nt),
        in_specs=[pl.BlockSpec((SUBLANES, tt, d), lambda gi, ti: (gi, ti, 0)),
                  _const_spec(mod8.shape), _const_spec((1, d)), _const_spec(w_in.shape)],
        out_specs=[row_spec(d), row_spec(d_lru), row_spec(d_lru), xs_spec],
        out_shape=[jax.ShapeDtypeStruct((g, t * SUBLANES, d), F32),
                   jax.ShapeDtypeStruct((g, t * SUBLANES, d_lru), BF16),
                   jax.ShapeDtypeStruct((g, t * SUBLANES, d_lru), BF16),
                   jax.ShapeDtypeStruct(xs_shape, F32)],
        compiler_params=_params(("arbitrary", "arbitrary")),
        name="stage1",
    )(x, mod8, g_pre.reshape(1, d), w_in)


def _lru_kernel(xm_ref, xp_ref, xn_ref, cw_ref, cb_ref, wr_ref, wi_ref, br_ref, bi_ref, lam_ref, h0_ref,
                h_ref, fin_ref, xe_scr, a_scr, b_scr, carry, *, reverse, nt):
    rows, c = xm_ref.shape
    tt = rows // SUBLANES
    i = pl.program_id(1)
    blk = (nt - 1 - i) if reverse else i

    @pl.when(i == 0)
    def _():
        carry[...] = h0_ref[...]

    halo = 2 * SUBLANES
    prev_ok = (blk > 0).astype(F32)
    next_ok = (blk < nt - 1).astype(F32)
    xe_scr[0:halo] = xp_ref[...].astype(F32) * prev_ok
    xe_scr[halo:halo + rows] = xm_ref[...].astype(F32)
    xe_scr[halo + rows:halo + rows + SUBLANES] = xn_ref[0:SUBLANES].astype(F32) * next_ok

    for j in range(c // MXU_TILE):
        cs = slice(j * MXU_TILE, (j + 1) * MXU_TILE)
        cc = (-0.5 * LRU_C * LOG2_E) * jax.nn.softplus(-lam_ref[:, cs])
        hu = 0.5 * cb_ref[:, cs]
        for k in range(4):
            hu = hu + (0.5 * cw_ref[k:k + 1, cs]) * xe_scr[k * SUBLANES:k * SUBLANES + rows, cs]
        hub = hu.astype(BF16)
        tr = jnp.tanh(_dot(hub, wr_ref[j]) + 0.5 * br_ref[:, cs])
        ti = jnp.tanh(_dot(hub, wi_ref[j]) + 0.5 * bi_ref[:, cs])
        a = jnp.exp2(cc * tr + cc)
        y = 1.0 - a * a
        a_scr[:, cs] = a
        b_scr[:, cs] = jnp.where(y > 0.0, y * lax.rsqrt(y), 0.0) * (hu * ti + hu)

    def step(s, h):
        t = (tt - 1 - s) if reverse else s
        sl = pl.ds(pl.multiple_of(t * SUBLANES, SUBLANES), SUBLANES)
        h = a_scr[sl, :] * h + b_scr[sl, :]
        b_scr[sl, :] = h
        return h

    h = lax.fori_loop(0, tt, step, carry[...], unroll=8)
    carry[...] = h
    fin_ref[...] = h
    h_ref[...] = b_scr[...].astype(BF16)


def _lru_call(xa, conv_w, conv_b, wr_bd, wi_bd, b_r, b_i, lam, h0, reverse):
    g, n, c = xa.shape
    tt = T_BLK
    rows = tt * SUBLANES
    nt = n // rows
    hb = 2 * SUBLANES
    per = rows // hb
    nhb = n // hb
    blk = (lambda ti: nt - 1 - ti) if reverse else (lambda ti: ti)
    main = pl.BlockSpec((None, rows, c), lambda gi, ti: (gi, blk(ti), 0))
    prev = pl.BlockSpec((None, hb, c), lambda gi, ti: (gi, jnp.maximum(blk(ti) * per - 1, 0), 0))
    nxt = pl.BlockSpec((None, hb, c), lambda gi, ti: (gi, jnp.minimum((blk(ti) + 1) * per, nhb - 1), 0))
    vec = lambda: _const_spec((1, c))
    return pl.pallas_call(
        functools.partial(_lru_kernel, reverse=reverse, nt=nt),
        grid=(g, nt),
        in_specs=[main, prev, nxt, _const_spec((4, c)), vec(), _const_spec(wr_bd.shape), _const_spec(wi_bd.shape),
                  vec(), vec(), vec(), pl.BlockSpec((None, SUBLANES, c), lambda gi, ti: (gi, 0, 0))],
        out_specs=[main, pl.BlockSpec((None, SUBLANES, c), lambda gi, ti: (gi, 0, 0))],
        out_shape=[jax.ShapeDtypeStruct((g, n, c), BF16), jax.ShapeDtypeStruct((g, SUBLANES, c), F32)],
        scratch_shapes=[pltpu.VMEM((rows + 4 * SUBLANES, c), F32), pltpu.VMEM((rows, c), F32),
                        pltpu.VMEM((rows, c), F32), pltpu.VMEM((SUBLANES, c), F32)],
        compiler_params=_params(("arbitrary", "arbitrary")),
        name="lru_bwd" if reverse else "lru_fwd",
    )(xa, xa, xa, conv_w, conv_b.reshape(1, c), wr_bd, wi_bd, b_r.reshape(1, c), b_i.reshape(1, c),
      lam.reshape(1, c), h0)


S5_CHUNK = 16
S5_T_BLK = 512
S5_PAIR_UNROLL = 4


def _s5_kernel(xs_ref, w1_ref, q_ref, ar_ref, ai_ref, h0r_ref, h0i_ref, y_ref, finr_ref, fini_ref,
               xt_scr, yt_scr, st_scr, hin_scr, cr, ci, *, reverse):
    nchunk = xs_ref.shape[0]
    ch = xs_ref.shape[-1]
    n = nchunk * SUBLANES
    ngrp = ch // S5_GROUP
    npair = ngrp // 2
    kdim = S5_CHUNK * S5_GROUP
    ps = S5_STATE
    i = pl.program_id(1)

    @pl.when(i == 0)
    def _():
        for j in range(npair):
            cr[j] = h0r_ref[:, 2 * ps * j:2 * ps * (j + 1)]
            ci[j] = h0i_ref[:, 2 * ps * j:2 * ps * (j + 1)]

    for tau in range(S5_CHUNK):
        x = xs_ref[:, tau].reshape(n, ch).astype(BF16)
        xt_scr[:, tau] = x.T.reshape(ngrp, S5_GROUP, n)

    def pair(j, u):
        g0 = 2 * j
        g1 = g0 + 1
        r0 = _dot(w1_ref[g0], xt_scr[g0].reshape(kdim, n))
        r1 = _dot(w1_ref[g1], xt_scr[g1].reshape(kdim, n))
        st = jnp.concatenate([r0[kdim:kdim + ps], r1[kdim:kdim + ps], r0[kdim + ps:], r1[kdim + ps:]], axis=0)
        st_scr[u] = st.astype(BF16).T.astype(F32)
        ar = jnp.broadcast_to(ar_ref[j], (SUBLANES, 2 * ps))
        ai = jnp.broadcast_to(ai_ref[j], (SUBLANES, 2 * ps))
        hr = cr[j]
        hi = ci[j]
        for s in range(nchunk):
            m = (nchunk - 1 - s) if reverse else s
            rs = slice(m * SUBLANES, (m + 1) * SUBLANES)
            hin_scr[u, rs, 0:2 * ps] = hr
            hin_scr[u, rs, 2 * ps:4 * ps] = hi
            sr = st_scr[u, rs, 0:2 * ps]
            si = st_scr[u, rs, 2 * ps:4 * ps]
            hr, hi = ar * hr - ai * hi + sr, ar * hi + ai * hr + si
        cr[j] = hr
        ci[j] = hi
        ht = hin_scr[u].astype(BF16).T
        h0t = jnp.concatenate([ht[0:ps], ht[2 * ps:3 * ps]], axis=0)
        h1t = jnp.concatenate([ht[ps:2 * ps], ht[3 * ps:4 * ps]], axis=0)
        y0 = r0[0:kdim] + _dot(q_ref[g0], h0t)
        y1 = r1[0:kdim] + _dot(q_ref[g1], h1t)
        yt_scr[g0] = y0.astype(BF16).reshape(S5_CHUNK, S5_GROUP, n)
        yt_scr[g1] = y1.astype(BF16).reshape(S5_CHUNK, S5_GROUP, n)

    def pairs(jj, carry_):
        for u in range(S5_PAIR_UNROLL):
            pair(jj * S5_PAIR_UNROLL + u, u)
        return carry_

    lax.fori_loop(0, npair // S5_PAIR_UNROLL, pairs, 0)

    for j in range(npair):
        finr_ref[:, 2 * ps * j:2 * ps * (j + 1)] = cr[j]
        fini_ref[:, 2 * ps * j:2 * ps * (j + 1)] = ci[j]
    for tau in range(S5_CHUNK):
        yt = yt_scr[:, tau].reshape(ch, n)
        y_ref[:, tau] = yt.T.astype(F32).reshape(nchunk, SUBLANES, ch)


def _s5_call(xs, w1, q, a16_re, a16_im, h0_re, h0_im, reverse):
    g, t, _, ch = xs.shape
    ngrp = ch // S5_GROUP
    ns = ngrp * S5_STATE
    tt = min(S5_T_BLK, t)
    nt = t // tt
    nchunk = tt // S5_CHUNK
    n = nchunk * SUBLANES
    xs5 = xs.reshape(g, t // S5_CHUNK, S5_CHUNK, SUBLANES, ch)
    blk = (lambda ti: nt - 1 - ti) if reverse else (lambda ti: ti)
    main = pl.BlockSpec((None, nchunk, S5_CHUNK, SUBLANES, ch), lambda gi, ti: (gi, blk(ti), 0, 0, 0))
    st = pl.BlockSpec((None, SUBLANES, ns), lambda gi, ti: (gi, 0, 0))
    y, fr, fi = pl.pallas_call(
        functools.partial(_s5_kernel, reverse=reverse),
        grid=(g, nt),
        in_specs=[main, _const_spec(w1.shape), _const_spec(q.shape), _const_spec(a16_re.shape),
                  _const_spec(a16_im.shape), st, st],
        out_specs=[main, st, st],
        out_shape=[jax.ShapeDtypeStruct(xs5.shape, F32), jax.ShapeDtypeStruct((g, SUBLANES, ns), F32),
                   jax.ShapeDtypeStruct((g, SUBLANES, ns), F32)],
        scratch_shapes=[pltpu.VMEM((ngrp, S5_CHUNK, S5_GROUP, n), BF16),
                        pltpu.VMEM((ngrp, S5_CHUNK, S5_GROUP, n), BF16),
                        pltpu.VMEM((S5_PAIR_UNROLL, n, 4 * S5_STATE), F32),
                        pltpu.VMEM((S5_PAIR_UNROLL, n, 4 * S5_STATE), F32),
                        pltpu.VMEM((ngrp // 2, SUBLANES, 2 * S5_STATE), F32),
                        pltpu.VMEM((ngrp // 2, SUBLANES, 2 * S5_STATE), F32)],
        compiler_params=_params(("arbitrary", "arbitrary")),
        name="s5_bwd" if reverse else "s5_fwd",
    )(xs5, w1, q, a16_re, a16_im, h0_re, h0_im)
    return y.reshape(xs.shape), fr, fi


def _s5_chunk_operators(a_re, a_im, log_dt, b_re, b_im, c_re, c_im, reverse):
    hi_p = lax.Precision.HIGHEST
    L = S5_CHUNK
    ngrp, ps, hh = b_re.shape
    dt = jnp.exp(log_dt)[:, None]
    k = jnp.arange(L + 1, dtype=F32)[:, None, None]
    mag = jnp.exp(k * (a_re * dt))
    ang = k * (a_im * dt)
    pw_re, pw_im = mag * jnp.cos(ang), mag * jnp.sin(ang)
    abar_re, abar_im = pw_re[1], pw_im[1]
    den = a_re * a_re + a_im * a_im
    nr, ni = abar_re - 1.0, abar_im
    f_re = (nr * a_re + ni * a_im) / den
    f_im = (ni * a_re - nr * a_im) / den
    bb_re = f_re[..., None] * b_re - f_im[..., None] * b_im
    bb_im = f_re[..., None] * b_im + f_im[..., None] * b_re
    d_re = pw_re[..., None] * bb_re - pw_im[..., None] * bb_im
    d_im = pw_re[..., None] * bb_im + pw_im[..., None] * bb_re
    kk = (jnp.einsum("ghp,kgpj->kghj", c_re, d_re[:L], precision=hi_p)
          - jnp.einsum("ghp,kgpj->kghj", c_im, d_im[:L], precision=hi_p))
    tau = jnp.arange(L)[:, None]
    sig = jnp.arange(L)[None, :]
    lag = (sig - tau) if reverse else (tau - sig)
    m = jnp.where((lag >= 0)[..., None, None, None], kk[jnp.clip(lag, 0, L - 1)], 0.0)
    m = m.transpose(2, 0, 3, 1, 4).reshape(ngrp, L * hh, L * hh)
    e_p = jnp.arange(L) if reverse else (L - 1 - jnp.arange(L))
    p_re = d_re[e_p].transpose(1, 2, 0, 3).reshape(ngrp, ps, L * hh)
    p_im = d_im[e_p].transpose(1, 2, 0, 3).reshape(ngrp, ps, L * hh)
    e_q = (L - jnp.arange(L)) if reverse else (jnp.arange(L) + 1)
    w_re = c_re[None] * pw_re[e_q][:, :, None, :] - c_im[None] * pw_im[e_q][:, :, None, :]
    w_im = c_re[None] * pw_im[e_q][:, :, None, :] + c_im[None] * pw_re[e_q][:, :, None, :]
    q = jnp.concatenate([w_re, -w_im], axis=-1).transpose(1, 0, 2, 3).reshape(ngrp, L * hh, 2 * ps)
    w1 = jnp.concatenate([m, p_re, p_im], axis=1)
    a16_re = pw_re[L].reshape(ngrp // 2, 1, 2 * ps)
    a16_im = pw_im[L].reshape(ngrp // 2, 1, 2 * ps)
    return w1.astype(BF16), q.astype(BF16), a16_re, a16_im


def _mix_kernel(x_ref, ga_ref, hf_ref, hb_ref, yf_ref, yb_ref, xs_ref, mod_ref, gpre_ref, gpost_ref,
                wg_ref, bg_ref, wpl_ref, wps_ref, wo_ref, d_ref, wglu_ref, bglu_ref, o_ref, *, colmajor):
    rows, d = x_ref.shape
    tt = rows // SUBLANES
    x3 = x_ref[...].reshape(tt, SUBLANES, d)
    sh1 = mod_ref[:, 0:d][None]
    sc1 = mod_ref[:, d:2 * d][None]
    g1 = mod_ref[:, 2 * d:3 * d][None]
    hn = (_rms(x3) * gpre_ref[...] * (1.0 + sc1) + sh1).reshape(rows, d).astype(BF16)
    gate = jax.nn.sigmoid(_dot(hn, wg_ref[...]) + bg_ref[...])

    ha = hf_ref[...].astype(F32) + hb_ref[...].astype(F32)
    ya = jax.nn.gelu(ga_ref[...].astype(F32)) * ha
    pa = _dot(ya.astype(BF16), wpl_ref[...])

    def s5_rows(ref):
        v = ref[:, 0] if colmajor else ref[...]
        return v.reshape(rows, v.shape[-1])

    us = s5_rows(xs_ref)
    hs = s5_rows(yf_ref) + s5_rows(yb_ref)
    vs = jax.nn.gelu(hs + d_ref[...] * us)
    ys = vs * jax.nn.sigmoid(_dot(vs.astype(BF16), wglu_ref[...]) + bglu_ref[...])
    pb = _dot(ys.astype(BF16), wps_ref[...])

    mm = gate[:, 0:d] * pa + gate[:, d:2 * d] * pb
    m = _dot(mm.astype(BF16), wo_ref[...])
    mn = (_rms(m) * gpost_ref[...]).reshape(tt, SUBLANES, d)
    o_ref[...] = (x3 + g1 * mn).reshape(rows, d)


def _mix_call(xtm, ga, hf, hb, yf, yb, xs, mod8, g_pre, g_post, w_gate, b_gate, w_pl, w_ps, w_out,
              s5_d, w_glu, b_glu, colmajor):
    g, n, d = xtm.shape
    tt = T_BLK
    rows = tt * SUBLANES
    nt = n // rows
    ch = xs.shape[-1]
    row_spec = lambda c: pl.BlockSpec((None, rows, c), lambda gi, ti: (gi, ti, 0))
    if colmajor:
        s5_spec = pl.BlockSpec((GRID_W, 1, SUBLANES, ch), lambda gi, ti: (0, ti, 0, 0))
    else:
        s5_spec = pl.BlockSpec((None, tt, SUBLANES, ch), lambda gi, ti: (gi, ti, 0, 0))
    dl = ga.shape[-1]
    consts = [mod8, g_pre.reshape(1, d), g_post.reshape(1, d), w_gate, b_gate.reshape(1, 2 * d), w_pl, w_ps,
              w_out, s5_d.reshape(1, ch), w_glu, b_glu.reshape(1, ch)]
    return pl.pallas_call(
        functools.partial(_mix_kernel, colmajor=colmajor),
        grid=(g, nt),
        in_specs=[row_spec(d), row_spec(dl), row_spec(dl), row_spec(dl), s5_spec, s5_spec, s5_spec]
                 + [_const_spec(a.shape) for a in consts],
        out_specs=row_spec(d),
        out_shape=jax.ShapeDtypeStruct((g, n, d), F32),
        compiler_params=_params(("arbitrary", "arbitrary")),
        name="mix",
    )(xtm, ga, hf, hb, yf, yb, xs, *consts)


FF_CHUNKS = 2


def _ffn_kernel(x_ref, mod_ref, gpre_ref, gpost_ref, w1_ref, w3_ref, w2_ref, o_ref):
    rows, d = x_ref.shape
    tt = rows // SUBLANES
    x3 = x_ref[...].reshape(tt, SUBLANES, d)
    sh2 = mod_ref[:, 3 * d:4 * d][None]
    sc2 = mod_ref[:, 4 * d:5 * d][None]
    g2 = mod_ref[:, 5 * d:6 * d][None]
    hn = (_rms(x3) * gpre_ref[...] * (1.0 + sc2) + sh2).reshape(rows, d).astype(BF16)
    dff = w1_ref.shape[-1]
    cw = dff // FF_CHUNKS
    f = jnp.zeros((rows, d), F32)
    for k in range(FF_CHUNKS):
        cs = slice(k * cw, (k + 1) * cw)
        u1 = _dot(hn, w1_ref[:, cs])
        u3 = _dot(hn, w3_ref[:, cs])
        act = (u1 * jax.nn.sigmoid(u1) * u3).astype(BF16)
        f = f + _dot(act, w2_ref[cs, :])
    fn = (_rms(f) * gpost_ref[...]).reshape(tt, SUBLANES, d)
    o_ref[...] = jnp.swapaxes(x3 + g2 * fn, 0, 1)


def _ffn_call(x1, mod8, g_pre, g_post, w1, w3, w2):
    g, n, d = x1.shape
    tt = T_BLK
    rows = tt * SUBLANES
    nt = n // rows
    consts = [mod8, g_pre.reshape(1, d), g_post.reshape(1, d), w1, w3, w2]
    return pl.pallas_call(
        _ffn_kernel,
        grid=(g, nt),
        in_specs=[pl.BlockSpec((None, rows, d), lambda gi, ti: (gi, ti, 0))]
                 + [_const_spec(a.shape) for a in consts],
        out_specs=pl.BlockSpec((SUBLANES, tt, d), lambda gi, ti: (gi, ti, 0)),
        out_shape=jax.ShapeDtypeStruct((g * SUBLANES, nt * tt, d), F32),
        compiler_params=_params(("arbitrary", "arbitrary")),
        name="ffn",
    )(x1, *consts)


def _block_diag_tiles(w, per_tile):
    nb, k, m = w.shape
    w = w.reshape(nb // per_tile, per_tile, k, m)
    eye = jnp.eye(per_tile, dtype=w.dtype)
    t = jnp.einsum("tbkm,bc->tbkcm", w, eye)
    return t.reshape(nb // per_tile, per_tile * k, per_tile * m)


def _layer(x, mod8, p, h0_lru, h0_re, h0_im, colmajor):
    b, t, d = x.shape
    d_lru = p["conv_b"].shape[-1]
    d_s5 = p["s5_d"].shape[-1]
    xtm, xa, ga, xs = _stage1_call(x, mod8, p["g_pre_mix"], p["w_in"], d_lru, d_s5, colmajor)
    if colmajor:
        xs_seq = xs.reshape(1, GRID_W * xs.shape[1], SUBLANES, d_s5)
    else:
        xs_seq = xs
    hs, fins = [], []
    ys, fr, fi = [], [], []
    for dr in range(2):
        h, fin = _lru_call(xa, p["conv_w"], p["conv_b"], p["wr_bd"][dr], p["wi_bd"][dr], p["lru_b_r"][dr],
                           p["lru_b_i"][dr], p["lru_lambda"][dr], h0_lru[dr], reverse=bool(dr))
        hs.append(h)
        fins.append(fin)
        y, r_, i_ = _s5_call(xs_seq, *p["s5_ops"][dr], h0_re[dr], h0_im[dr], reverse=bool(dr))
        ys.append(y.reshape(xs.shape))
        fr.append(r_)
        fi.append(i_)
    x1 = _mix_call(xtm, ga, hs[0], hs[1], ys[0], ys[1], xs, mod8, p["g_pre_mix"], p["g_post_mix"], p["w_gate"],
                   p["b_gate"], p["w_proj_lru"], p["w_proj_s5"], p["w_out"], p["s5_d"], p["s5_w_glu"],
                   p["s5_b_glu"], colmajor)
    out = _ffn_call(x1, mod8, p["g_pre_ffn"], p["g_post_ffn"], p["w_ff1"], p["w_ff3"], p["w_ff2"])
    return out, fins, fr, fi


def kernel(x_prompt, x_sample, c, state_lru, state_s5_re, state_s5_im, c_ctx, w_mod, b_mod, g_pre_mix, g_post_mix,
           g_pre_ffn, g_post_ffn, w_in, conv_w, conv_b, lru_w_r, lru_b_r, lru_w_i, lru_b_i, lru_lambda,
           s5_a_re, s5_a_im, s5_log_dt, s5_b_re, s5_b_im, s5_c_re, s5_c_im, s5_d, s5_w_glu, s5_b_glu,
           w_proj_lru, w_proj_s5, w_gate, b_gate, w_out, w_ff_in, w_ff_out):
    depth = w_mod.shape[0]
    n_ctx, _, d = x_prompt.shape
    n_dec = x_sample.shape[0]
    assert n_dec == SUBLANES and n_ctx % SUBLANES == 0
    g_ctx = n_ctx // SUBLANES
    n_dir, n_grp, n_st = s5_a_re.shape[1:]
    d_lru = conv_b.shape[-1]
    d_ff = w_ff_out.shape[1]
    heads_per_tile = MXU_TILE // lru_w_r.shape[-1]

    y_prompt, y_sample = x_prompt, x_sample
    lru_list, re_list, im_list = [], [], []
    for l in range(depth):
        cvecs = jnp.concatenate([c, c_ctx[None], jnp.zeros((SUBLANES - 1, d), F32)], axis=0)
        mod = _mod_call(cvecs, w_mod[l], b_mod[l])
        mod_dec = mod[0:SUBLANES]
        mod_ctx = jnp.broadcast_to(mod[SUBLANES:SUBLANES + 1], (SUBLANES, mod.shape[1]))

        p = {
            "g_pre_mix": g_pre_mix[l], "g_post_mix": g_post_mix[l], "g_pre_ffn": g_pre_ffn[l],
            "g_post_ffn": g_post_ffn[l], "w_in": w_in[l].astype(BF16), "conv_w": conv_w[l], "conv_b": conv_b[l],
            "wr_bd": [_block_diag_tiles(lru_w_r[l, dr], heads_per_tile).astype(BF16) for dr in range(n_dir)],
            "wi_bd": [_block_diag_tiles(lru_w_i[l, dr], heads_per_tile).astype(BF16) for dr in range(n_dir)],
            "lru_b_r": lru_b_r[l], "lru_b_i": lru_b_i[l],
            "lru_lambda": lru_lambda[l],
            "s5_ops": [_s5_chunk_operators(s5_a_re[l, dr], s5_a_im[l, dr], s5_log_dt[l, dr], s5_b_re[l, dr],
                                           s5_b_im[l, dr], s5_c_re[l, dr], s5_c_im[l, dr], reverse=bool(dr))
                       for dr in range(n_dir)],
            "s5_d": s5_d[l], "s5_w_glu": s5_w_glu[l].astype(BF16), "s5_b_glu": s5_b_glu[l],
            "w_proj_lru": w_proj_lru[l].astype(BF16), "w_proj_s5": w_proj_s5[l].astype(BF16),
            "w_gate": w_gate[l].astype(BF16), "b_gate": b_gate[l], "w_out": w_out[l].astype(BF16),
            "w_ff1": w_ff_in[l, :, :d_ff].astype(BF16), "w_ff3": w_ff_in[l, :, d_ff:].astype(BF16),
            "w_ff2": w_ff_out[l].astype(BF16),
        }
        ns = n_grp * n_st
        zero_lru = [jnp.zeros((g_ctx, SUBLANES, d_lru), F32)] * n_dir
        zero_s5 = [jnp.zeros((g_ctx, SUBLANES, ns), F32)] * n_dir
        y_prompt, f_lru, f_re, f_im = _layer(y_prompt, mod_ctx, p, zero_lru, zero_s5, zero_s5, colmajor=False)
        lru_list.append(jnp.stack([f.reshape(n_ctx, d_lru) for f in f_lru], axis=1))
        re_list.append(jnp.stack([f.reshape(n_ctx, n_grp, n_st) for f in f_re], axis=1))
        im_list.append(jnp.stack([f.reshape(n_ctx, n_grp, n_st) for f in f_im], axis=1))

        h0_lru = [state_lru[:, l, dr].astype(F32)[None] for dr in range(n_dir)]
        h0_re = [state_s5_re[:, l, dr].reshape(1, n_dec, ns) for dr in range(n_dir)]
        h0_im = [state_s5_im[:, l, dr].reshape(1, n_dec, ns) for dr in range(n_dir)]
        y_sample, _, _, _ = _layer(y_sample, mod_dec, p, h0_lru, h0_re, h0_im, colmajor=True)
    return (y_prompt, y_sample, jnp.stack(lru_list, axis=1), jnp.stack(re_list, axis=1),
            jnp.stack(im_list, axis=1))
```

```python
import functools

import jax
import jax.numpy as jnp
from jax import lax
from jax.experimental import pallas as pl
from jax.experimental.pallas import tpu as pltpu

F32 = jnp.float32
BF16 = jnp.bfloat16

EPS = 1e-6
LRU_C = 8.0
LOG2_E = 1.4426950408889634
GRID_W = 64
SUBLANES = 8
MXU_TILE = 256
S5_GROUP = 16
S5_STATE = 64
VMEM_LIMIT = 56 * 1024 * 1024

T_BLK = 64


def _const_spec(shape):
    nd = len(shape)
    return pl.BlockSpec(shape, lambda *_: (0,) * nd, pipeline_mode=pl.Buffered(1))


def _params(sem):
    return pltpu.CompilerParams(dimension_semantics=sem, vmem_limit_bytes=VMEM_LIMIT)


def _dot(a, b):
    return jnp.dot(a, b, preferred_element_type=F32)


def _rms(x):
    return x * lax.rsqrt(jnp.mean(x * x, axis=-1, keepdims=True) + EPS)


def _mod_kernel(c_ref, w_ref, b_ref, o_ref):
    c = c_ref[...]
    s = c * jax.nn.sigmoid(c)
    o_ref[...] = jnp.dot(s, w_ref[...], preferred_element_type=F32,
                         precision=lax.Precision.HIGHEST) + b_ref[...]


def _mod_call(cvecs, w_mod, b_mod):
    rows, d = cvecs.shape
    n = w_mod.shape[1]
    nb = 4
    return pl.pallas_call(
        _mod_kernel,
        grid=(nb,),
        in_specs=[pl.BlockSpec((rows, d), lambda j: (0, 0)),
                  pl.BlockSpec((d, n // nb), lambda j: (0, j)),
                  pl.BlockSpec((1, n // nb), lambda j: (0, j))],
        out_specs=pl.BlockSpec((rows, n // nb), lambda j: (0, j)),
        out_shape=jax.ShapeDtypeStruct((rows, n), F32),
        compiler_params=_params(("arbitrary",)),
        name="mod",
    )(cvecs, w_mod, b_mod.reshape(1, n))


def _stage1_kernel(x_ref, mod_ref, g_ref, w_ref, xtm_ref, xa_ref, ga_ref, xs_ref, *, colmajor):
    _, tt, d = x_ref.shape
    rows = tt * SUBLANES
    x = jnp.swapaxes(x_ref[...], 0, 1)
    sh1 = mod_ref[:, 0:d][None]
    sc1 = mod_ref[:, d:2 * d][None]
    hn = _rms(x) * g_ref[...] * (1.0 + sc1) + sh1
    xtm_ref[...] = x.reshape(rows, d)
    z = _dot(hn.reshape(rows, d).astype(BF16), w_ref[...])
    dl = xa_ref.shape[-1]
    xa_ref[...] = z[:, 0:dl].astype(BF16)
    ga_ref[...] = z[:, dl:2 * dl].astype(BF16)
    zs = z[:, 2 * dl:]
    if colmajor:
        xs_ref[:, 0] = zs.reshape(tt, SUBLANES, zs.shape[-1])
    else:
        xs_ref[...] = zs.reshape(tt, SUBLANES, zs.shape[-1])


def _stage1_call(x, mod8, g_pre, w_in, d_lru, d_s5, colmajor):
    b, t, d = x.shape
    g = b // SUBLANES
    tt = T_BLK
    nt = t // tt
    rows = tt * SUBLANES
    if colmajor:
        assert g == 1 and tt == GRID_W
        xs_shape = (GRID_W, nt, SUBLANES, d_s5)
        xs_spec = pl.BlockSpec((GRID_W, 1, SUBLANES, d_s5), lambda gi, ti: (0, ti, 0, 0))
    else:
        xs_shape = (g, t, SUBLANES, d_s5)
        xs_spec = pl.BlockSpec((None, tt, SUBLANES, d_s5), lambda gi, ti: (gi, ti, 0, 0))
    row_spec = lambda c: pl.BlockSpec((None, rows, c), lambda gi, ti: (gi, ti, 0))
    return pl.pallas_call(
        functools.partial(_stage1_kernel, colmajor=colmajor),
        grid=(g, nt),
        in_specs=[pl.BlockSpec((SUBLANES, tt, d), lambda gi, ti: (gi, ti, 0)),
                  _const_spec(mod8.shape), _const_spec((1, d)), _const_spec(w_in.shape)],
        out_specs=[row_spec(d), row_spec(d_lru), row_spec(d_lru), xs_spec],
        out_shape=[jax.ShapeDtypeStruct((g, t * SUBLANES, d), F32),
                   jax.ShapeDtypeStruct((g, t * SUBLANES, d_lru), BF16),
                   jax.ShapeDtypeStruct((g, t * SUBLANES, d_lru), BF16),
                   jax.ShapeDtypeStruct(xs_shape, F32)],
        compiler_params=_params(("arbitrary", "arbitrary")),
        name="stage1",
    )(x, mod8, g_pre.reshape(1, d), w_in)


def _lru_kernel(*refs, reverse, nt, from_conv):
    if from_conv:
        (xm_ref, xp_ref, xn_ref, cw_ref, cb_ref, wr_ref, wi_ref, br_ref, bi_ref, lam_ref, h0_ref,
         h_ref, fin_ref, hu_out_ref, xe_scr, a_scr, b_scr, carry) = refs
        rows, c = xm_ref.shape
    else:
        (hu_ref, wr_ref, wi_ref, br_ref, bi_ref, lam_ref, h0_ref, h_ref, fin_ref, a_scr, b_scr, carry) = refs
        rows, c = hu_ref.shape
    tt = rows // SUBLANES
    i = pl.program_id(1)
    blk = (nt - 1 - i) if reverse else i

    @pl.when(i == 0)
    def _():
        carry[...] = h0_ref[...]

    if from_conv:
        halo = 2 * SUBLANES
        prev_ok = (blk > 0).astype(F32)
        next_ok = (blk < nt - 1).astype(F32)
        xe_scr[0:halo] = xp_ref[...].astype(F32) * prev_ok
        xe_scr[halo:halo + rows] = xm_ref[...].astype(F32)
        xe_scr[halo + rows:halo + rows + SUBLANES] = xn_ref[0:SUBLANES].astype(F32) * next_ok

    for j in range(c // MXU_TILE):
        cs = slice(j * MXU_TILE, (j + 1) * MXU_TILE)
        cc = (-0.5 * LRU_C * LOG2_E) * jax.nn.softplus(-lam_ref[:, cs])
        if from_conv:
            hu = 0.5 * cb_ref[:, cs]
            for k in range(4):
                hu = hu + (0.5 * cw_ref[k:k + 1, cs]) * xe_scr[k * SUBLANES:k * SUBLANES + rows, cs]
            hub = hu.astype(BF16)
            hu_out_ref[:, cs] = hub
        else:
            hub = hu_ref[:, cs]
            hu = hub.astype(F32)
        tr = jnp.tanh(_dot(hub, wr_ref[j]) + 0.5 * br_ref[:, cs])
        ti = jnp.tanh(_dot(hub, wi_ref[j]) + 0.5 * bi_ref[:, cs])
        a = jnp.exp2(cc * tr + cc)
        y = 1.0 - a * a
        a_scr[:, cs] = a
        b_scr[:, cs] = jnp.where(y > 0.0, y * lax.rsqrt(y), 0.0) * (hu * ti + hu)

    def step(s, h):
        t = (tt - 1 - s) if reverse else s
        sl = pl.ds(pl.multiple_of(t * SUBLANES, SUBLANES), SUBLANES)
        h = a_scr[sl, :] * h + b_scr[sl, :]
        b_scr[sl, :] = h
        return h

    h = lax.fori_loop(0, tt, step, carry[...], unroll=8)
    carry[...] = h
    fin_ref[...] = h
    h_ref[...] = b_scr[...].astype(BF16)


def _lru_call(x, conv, wr_bd, wi_bd, b_r, b_i, lam, h0, reverse):
    g, n, c = x.shape
    tt = T_BLK
    rows = tt * SUBLANES
    nt = n // rows
    blk = (lambda ti: nt - 1 - ti) if reverse else (lambda ti: ti)
    main = pl.BlockSpec((None, rows, c), lambda gi, ti: (gi, blk(ti), 0))
    state = pl.BlockSpec((None, SUBLANES, c), lambda gi, ti: (gi, 0, 0))
    vec = lambda: _const_spec((1, c))
    gate_specs = [_const_spec(wr_bd.shape), _const_spec(wi_bd.shape), vec(), vec(), vec(), state]
    gate_args = (wr_bd, wi_bd, b_r.reshape(1, c), b_i.reshape(1, c), lam.reshape(1, c), h0)
    out_specs = [main, state]
    out_shape = [jax.ShapeDtypeStruct((g, n, c), BF16), jax.ShapeDtypeStruct((g, SUBLANES, c), F32)]
    scratch = [pltpu.VMEM((rows, c), F32), pltpu.VMEM((rows, c), F32), pltpu.VMEM((SUBLANES, c), F32)]
    if conv is not None:
        hb = 2 * SUBLANES
        per = rows // hb
        nhb = n // hb
        prev = pl.BlockSpec((None, hb, c), lambda gi, ti: (gi, jnp.maximum(blk(ti) * per - 1, 0), 0))
        nxt = pl.BlockSpec((None, hb, c), lambda gi, ti: (gi, jnp.minimum((blk(ti) + 1) * per, nhb - 1), 0))
        in_specs = [main, prev, nxt, _const_spec((4, c)), vec()] + gate_specs
        args = (x, x, x, conv[0], conv[1].reshape(1, c)) + gate_args
        out_specs = out_specs + [main]
        out_shape = out_shape + [jax.ShapeDtypeStruct((g, n, c), BF16)]
        scratch = [pltpu.VMEM((rows + 4 * SUBLANES, c), F32)] + scratch
    else:
        in_specs = [main] + gate_specs
        args = (x,) + gate_args
    return pl.pallas_call(
        functools.partial(_lru_kernel, reverse=reverse, nt=nt, from_conv=conv is not None),
        grid=(g, nt),
        in_specs=in_specs,
        out_specs=out_specs,
        out_shape=out_shape,
        scratch_shapes=scratch,
        compiler_params=_params(("arbitrary", "arbitrary")),
        name="lru_bwd" if reverse else "lru_fwd",
    )(*args)


S5_CHUNK = 16
S5_T_BLK = 512
S5_PAIR_UNROLL = 8


def _s5_kernel(xs_ref, m_ref, p_ref, q_ref, ar_ref, ai_ref, h0r_ref, h0i_ref, y_ref, finr_ref, fini_ref,
               xy_scr, st_scr, hin_scr, cr, ci, *, reverse):
    nseq, nchunk = xs_ref.shape[0], xs_ref.shape[1]
    ch = xs_ref.shape[-1]
    n = nseq * nchunk * SUBLANES
    ngrp = ch // S5_GROUP
    npair = ngrp // 2
    kdim = S5_CHUNK * S5_GROUP
    ps = S5_STATE
    unroll = S5_PAIR_UNROLL
    i = pl.program_id(1)

    @pl.when(i == 0)
    def _():
        for j in range(npair):
            for sq in range(nseq):
                cr[j, sq] = h0r_ref[sq, :, 2 * ps * j:2 * ps * (j + 1)]
                ci[j, sq] = h0i_ref[sq, :, 2 * ps * j:2 * ps * (j + 1)]

    for tau in range(S5_CHUNK):
        x = xs_ref[:, :, tau].reshape(n, ch).astype(BF16)
        xy_scr[:, tau] = x.T.reshape(ngrp, S5_GROUP, n)

    def pairs(jj, carry_):
        base = jj * unroll
        rs = [_dot(jnp.concatenate([m_ref[2 * base + e], p_ref[2 * base + e]], axis=0),
                   xy_scr[2 * base + e].reshape(kdim, n)) for e in range(2 * unroll)]
        for u in range(unroll):
            r0, r1 = rs[2 * u], rs[2 * u + 1]
            st = jnp.concatenate([r0[kdim:kdim + ps], r1[kdim:kdim + ps], r0[kdim + ps:], r1[kdim + ps:]], axis=0)
            st_scr[u] = st.astype(BF16).T.astype(F32)
        ars = [jnp.broadcast_to(ar_ref[base + u], (SUBLANES, 2 * ps)) for u in range(unroll)]
        ais = [jnp.broadcast_to(ai_ref[base + u], (SUBLANES, 2 * ps)) for u in range(unroll)]
        state = [[(cr[base + u, sq], ci[base + u, sq]) for sq in range(nseq)] for u in range(unroll)]
        for s in range(nchunk):
            m = (nchunk - 1 - s) if reverse else s
            for u in range(unroll):
                for sq in range(nseq):
                    row = (sq * nchunk + m) * SUBLANES
                    rows = slice(row, row + SUBLANES)
                    hr, hi = state[u][sq]
                    hin_scr[u, rows, 0:2 * ps] = hr
                    hin_scr[u, rows, 2 * ps:4 * ps] = hi
                    sr = st_scr[u, rows, 0:2 * ps]
                    si = st_scr[u, rows, 2 * ps:4 * ps]
                    state[u][sq] = (ars[u] * hr - ais[u] * hi + sr, ars[u] * hi + ais[u] * hr + si)
        for u in range(unroll):
            for sq in range(nseq):
                cr[base + u, sq], ci[base + u, sq] = state[u][sq]
        hts = [hin_scr[u].astype(BF16).T for u in range(unroll)]
        for u in range(unroll):
            ht = hts[u]
            for e in range(2):
                het = jnp.concatenate([ht[e * ps:(e + 1) * ps], ht[(2 + e) * ps:(3 + e) * ps]], axis=0)
                y = rs[2 * u + e][0:kdim] + _dot(q_ref[2 * (base + u) + e], het)
                xy_scr[2 * (base + u) + e] = y.astype(BF16).reshape(S5_CHUNK, S5_GROUP, n)
        return carry_

    lax.fori_loop(0, npair // unroll, pairs, 0)

    for j in range(npair):
        for sq in range(nseq):
            finr_ref[sq, :, 2 * ps * j:2 * ps * (j + 1)] = cr[j, sq]
            fini_ref[sq, :, 2 * ps * j:2 * ps * (j + 1)] = ci[j, sq]
    for tau in range(S5_CHUNK):
        yt = xy_scr[:, tau].reshape(ch, n)
        y_ref[:, :, tau] = yt.T.astype(F32).reshape(nseq, nchunk, SUBLANES, ch)


def _s5_call(xs, m, p, q, a16_re, a16_im, h0_re, h0_im, reverse):
    g, t, _, ch = xs.shape
    ngrp = ch // S5_GROUP
    ns = ngrp * S5_STATE
    tt = min(S5_T_BLK, t)
    nt = t // tt
    nseq = min(g, S5_T_BLK // tt)
    nchunk = tt // S5_CHUNK
    n = nseq * nchunk * SUBLANES
    xs5 = xs.reshape(g, t // S5_CHUNK, S5_CHUNK, SUBLANES, ch)
    blk = (lambda ti: nt - 1 - ti) if reverse else (lambda ti: ti)
    main = pl.BlockSpec((nseq, nchunk, S5_CHUNK, SUBLANES, ch), lambda gi, ti: (gi, blk(ti), 0, 0, 0))
    st = pl.BlockSpec((nseq, SUBLANES, ns), lambda gi, ti: (gi, 0, 0))
    y, fr, fi = pl.pallas_call(
        functools.partial(_s5_kernel, reverse=reverse),
        grid=(g // nseq, nt),
        in_specs=[main, _const_spec(m.shape), _const_spec(p.shape), _const_spec(q.shape), _const_spec(a16_re.shape),
                  _const_spec(a16_im.shape), st, st],
        out_specs=[main, st, st],
        out_shape=[jax.ShapeDtypeStruct(xs5.shape, F32), jax.ShapeDtypeStruct((g, SUBLANES, ns), F32),
                   jax.ShapeDtypeStruct((g, SUBLANES, ns), F32)],
        scratch_shapes=[pltpu.VMEM((ngrp, S5_CHUNK, S5_GROUP, n), BF16),
                        pltpu.VMEM((S5_PAIR_UNROLL, n, 4 * S5_STATE), F32),
                        pltpu.VMEM((S5_PAIR_UNROLL, n, 4 * S5_STATE), F32),
                        pltpu.VMEM((ngrp // 2, nseq, SUBLANES, 2 * S5_STATE), F32),
                        pltpu.VMEM((ngrp // 2, nseq, SUBLANES, 2 * S5_STATE), F32)],
        compiler_params=_params(("arbitrary", "arbitrary")),
        name="s5_bwd" if reverse else "s5_fwd",
    )(xs5, m, p, q, a16_re, a16_im, h0_re, h0_im)
    return y.reshape(xs.shape), fr, fi


def _s5_chunk_operators(a_re, a_im, log_dt, b_re, b_im, c_re, c_im, reverse):
    hi_p = lax.Precision.HIGHEST
    L = S5_CHUNK
    ngrp, ps, hh = b_re.shape
    dt = jnp.exp(log_dt)[:, None]
    lr, li = a_re * dt, a_im * dt

    def powers(e, xr, xi):
        mag = jnp.exp(e * xr)
        return mag * jnp.cos(e * xi), mag * jnp.sin(e * xi)

    abar_re, abar_im = powers(1.0, lr, li)
    den = a_re * a_re + a_im * a_im
    nr, ni = abar_re - 1.0, abar_im
    f_re = (nr * a_re + ni * a_im) / den
    f_im = (ni * a_re - nr * a_im) / den
    bb_re = f_re[..., None] * b_re - f_im[..., None] * b_im
    bb_im = f_re[..., None] * b_im + f_im[..., None] * b_re

    steps = jnp.arange(L, dtype=F32)
    pk_re, pk_im = powers(steps[None, :, None], lr[:, None, :], li[:, None, :])
    d_re = pk_re[..., None] * bb_re[:, None] - pk_im[..., None] * bb_im[:, None]
    d_im = pk_re[..., None] * bb_im[:, None] + pk_im[..., None] * bb_re[:, None]
    kk = (jnp.einsum("gip,gkpj->gijk", c_re, d_re, precision=hi_p)
          - jnp.einsum("gip,gkpj->gijk", c_im, d_im, precision=hi_p))
    tau = jnp.arange(L)[:, None]
    sig = jnp.arange(L)[None, :]
    lag = (sig - tau) if reverse else (tau - sig)
    onehot = (lag[:, :, None] == jnp.arange(L)[None, None, :]).astype(F32)
    m = jnp.sum(onehot[None, :, None, :, None, :] * kk[:, None, :, None, :, :], axis=-1)
    m = m.reshape(ngrp, L * hh, L * hh)

    lr2 = jnp.concatenate([lr, lr], axis=-1)
    li2 = jnp.concatenate([li, li], axis=-1)
    e_p = steps if reverse else (L - 1.0 - steps)
    pr2, pi2 = powers(e_p[None, None, :], lr2[:, :, None], li2[:, :, None])
    pa = jnp.concatenate([bb_re, bb_im], axis=1)
    pb = jnp.concatenate([-bb_im, bb_re], axis=1)
    p = (pr2[..., None] * pa[:, :, None, :] + pi2[..., None] * pb[:, :, None, :]).reshape(ngrp, 2 * ps, L * hh)

    e_q = (L - steps) if reverse else (steps + 1.0)
    qr2, qi2 = powers(e_q[None, :, None], lr2[:, None, :], li2[:, None, :])
    qa = jnp.concatenate([c_re, -c_im], axis=-1)
    qb = jnp.concatenate([-c_im, -c_re], axis=-1)
    q = (qa[:, None] * qr2[:, :, None, :] + qb[:, None] * qi2[:, :, None, :]).reshape(ngrp, L * hh, 2 * ps)

    a16_re, a16_im = powers(float(L), lr, li)
    return (m.astype(BF16), p.astype(BF16), q.astype(BF16), a16_re.reshape(ngrp // 2, 1, 2 * ps),
            a16_im.reshape(ngrp // 2, 1, 2 * ps))


def _mix_kernel(x_ref, ga_ref, hf_ref, hb_ref, yf_ref, yb_ref, xs_ref, mod_ref, gpre_ref, gpost_ref,
                wg_ref, bg_ref, wpl_ref, wps_ref, wo_ref, d_ref, wglu_ref, bglu_ref, o_ref, *, colmajor):
    rows, d = x_ref.shape
    tt = rows // SUBLANES
    x3 = x_ref[...].reshape(tt, SUBLANES, d)
    sh1 = mod_ref[:, 0:d][None]
    sc1 = mod_ref[:, d:2 * d][None]
    g1 = mod_ref[:, 2 * d:3 * d][None]
    hn = (_rms(x3) * gpre_ref[...] * (1.0 + sc1) + sh1).reshape(rows, d).astype(BF16)
    zg = _dot(hn, wg_ref[...])

    ha = hf_ref[...].astype(F32) + hb_ref[...].astype(F32)
    ya = jax.nn.gelu(ga_ref[...].astype(F32)) * ha
    pa = _dot(ya.astype(BF16), wpl_ref[...])

    def s5_rows(ref):
        v = ref[:, 0] if colmajor else ref[...]
        return v.reshape(rows, v.shape[-1])

    us = s5_rows(xs_ref)
    hs = s5_rows(yf_ref) + s5_rows(yb_ref)
    vs = jax.nn.gelu(hs + d_ref[...] * us)
    zglu = _dot(vs.astype(BF16), wglu_ref[...])
    tg = jnp.tanh(zg + bg_ref[...])
    ys = vs * jax.nn.sigmoid(zglu + bglu_ref[...])
    pb = _dot(ys.astype(BF16), wps_ref[...])

    mm = (pa + tg[:, 0:d] * pa) + (pb + tg[:, d:2 * d] * pb)
    m = _dot(mm.astype(BF16), wo_ref[...])
    mn = (_rms(m) * gpost_ref[...]).reshape(tt, SUBLANES, d)
    o_ref[...] = (x3 + g1 * mn).reshape(rows, d)


def _mix_call(xtm, ga, hf, hb, yf, yb, xs, mod8, g_pre, g_post, w_gate, b_gate, w_pl, w_ps, w_out,
              s5_d, w_glu, b_glu, colmajor):
    g, n, d = xtm.shape
    tt = T_BLK
    rows = tt * SUBLANES
    nt = n // rows
    ch = xs.shape[-1]
    row_spec = lambda c: pl.BlockSpec((None, rows, c), lambda gi, ti: (gi, ti, 0))
    if colmajor:
        s5_spec = pl.BlockSpec((GRID_W, 1, SUBLANES, ch), lambda gi, ti: (0, ti, 0, 0))
    else:
        s5_spec = pl.BlockSpec((None, tt, SUBLANES, ch), lambda gi, ti: (gi, ti, 0, 0))
    dl = ga.shape[-1]
    consts = [mod8, g_pre.reshape(1, d), g_post.reshape(1, d), w_gate, 0.5 * b_gate.reshape(1, 2 * d), w_pl, w_ps,
              w_out, s5_d.reshape(1, ch), w_glu, b_glu.reshape(1, ch)]
    return pl.pallas_call(
        functools.partial(_mix_kernel, colmajor=colmajor),
        grid=(g, nt),
        in_specs=[row_spec(d), row_spec(dl), row_spec(dl), row_spec(dl), s5_spec, s5_spec, s5_spec]
                 + [_const_spec(a.shape) for a in consts],
        out_specs=row_spec(d),
        out_shape=jax.ShapeDtypeStruct((g, n, d), F32),
        compiler_params=_params(("arbitrary", "arbitrary")),
        name="mix",
    )(xtm, ga, hf, hb, yf, yb, xs, *consts)


FF_CHUNKS = 11


def _ffn_kernel(x_ref, mod_ref, gpre_ref, gpost_ref, w1_ref, w3_ref, w2_ref, o_ref):
    rows, d = x_ref.shape
    tt = rows // SUBLANES
    x3 = x_ref[...].reshape(tt, SUBLANES, d)
    sh2 = mod_ref[:, 3 * d:4 * d][None]
    sc2 = mod_ref[:, 4 * d:5 * d][None]
    g2 = mod_ref[:, 5 * d:6 * d][None]
    hn = (_rms(x3) * gpre_ref[...] * (1.0 + sc2) + sh2).reshape(rows, d).astype(BF16)
    dff = w1_ref.shape[-1]
    cw = dff // FF_CHUNKS
    f = jnp.zeros((rows, d), F32)
    for k in range(FF_CHUNKS):
        cs = slice(k * cw, (k + 1) * cw)
        u1 = _dot(hn, w1_ref[:, cs])
        u3 = _dot(hn, w3_ref[:, cs])
        act = (u1 * jax.nn.sigmoid(u1) * u3).astype(BF16)
        f = f + _dot(act, w2_ref[cs, :])
    fn = (_rms(f) * gpost_ref[...]).reshape(tt, SUBLANES, d)
    o_ref[...] = jnp.swapaxes(x3 + g2 * fn, 0, 1)


def _ffn_call(x1, mod8, g_pre, g_post, w1, w3, w2):
    g, n, d = x1.shape
    tt = T_BLK
    rows = tt * SUBLANES
    nt = n // rows
    consts = [mod8, g_pre.reshape(1, d), g_post.reshape(1, d), w1, w3, w2]
    return pl.pallas_call(
        _ffn_kernel,
        grid=(g, nt),
        in_specs=[pl.BlockSpec((None, rows, d), lambda gi, ti: (gi, ti, 0))]
                 + [_const_spec(a.shape) for a in consts],
        out_specs=pl.BlockSpec((SUBLANES, tt, d), lambda gi, ti: (gi, ti, 0)),
        out_shape=jax.ShapeDtypeStruct((g * SUBLANES, nt * tt, d), F32),
        compiler_params=_params(("arbitrary", "arbitrary")),
        name="ffn",
    )(x1, *consts)


def _block_diag_tiles(w, per_tile):
    nb, k, m = w.shape
    w = w.reshape(nb // per_tile, per_tile, k, m)
    eye = jnp.eye(per_tile, dtype=w.dtype)
    t = jnp.einsum("tbkm,bc->tbkcm", w, eye)
    return t.reshape(nb // per_tile, per_tile * k, per_tile * m)


def _layer(x, mod8, p, h0_lru, h0_re, h0_im, colmajor):
    b, t, d = x.shape
    d_lru = p["conv_b"].shape[-1]
    d_s5 = p["s5_d"].shape[-1]
    xtm, xa, ga, xs = _stage1_call(x, mod8, p["g_pre_mix"], p["w_in"], d_lru, d_s5, colmajor)
    if colmajor:
        xs_seq = xs.reshape(1, GRID_W * xs.shape[1], SUBLANES, d_s5)
    else:
        xs_seq = xs
    hs, fins = [], []
    ys, fr, fi = [], [], []
    hu = None
    for dr in range(2):
        res = _lru_call(xa if dr == 0 else hu, (p["conv_w"], p["conv_b"]) if dr == 0 else None, p["wr_bd"][dr],
                        p["wi_bd"][dr], p["lru_b_r"][dr], p["lru_b_i"][dr], p["lru_lambda"][dr], h0_lru[dr],
                        reverse=bool(dr))
        if dr == 0:
            hu = res[2]
        hs.append(res[0])
        fins.append(res[1])
        y, r_, i_ = _s5_call(xs_seq, *p["s5_ops"][dr], h0_re[dr], h0_im[dr], reverse=bool(dr))
        ys.append(y.reshape(xs.shape))
        fr.append(r_)
        fi.append(i_)
    x1 = _mix_call(xtm, ga, hs[0], hs[1], ys[0], ys[1], xs, mod8, p["g_pre_mix"], p["g_post_mix"], p["w_gate"],
                   p["b_gate"], p["w_proj_lru"], p["w_proj_s5"], p["w_out"], p["s5_d"], p["s5_w_glu"],
                   p["s5_b_glu"], colmajor)
    out = _ffn_call(x1, mod8, p["g_pre_ffn"], p["g_post_ffn"], p["w_ff1"], p["w_ff3"], p["w_ff2"])
    return out, fins, fr, fi


def kernel(x_prompt, x_sample, c, state_lru, state_s5_re, state_s5_im, c_ctx, w_mod, b_mod, g_pre_mix, g_post_mix,
           g_pre_ffn, g_post_ffn, w_in, conv_w, conv_b, lru_w_r, lru_b_r, lru_w_i, lru_b_i, lru_lambda,
           s5_a_re, s5_a_im, s5_log_dt, s5_b_re, s5_b_im, s5_c_re, s5_c_im, s5_d, s5_w_glu, s5_b_glu,
           w_proj_lru, w_proj_s5, w_gate, b_gate, w_out, w_ff_in, w_ff_out):
    depth = w_mod.shape[0]
    n_ctx, _, d = x_prompt.shape
    n_dec = x_sample.shape[0]
    assert n_dec == SUBLANES and n_ctx % SUBLANES == 0
    g_ctx = n_ctx // SUBLANES
    n_dir, n_grp, n_st = s5_a_re.shape[1:]
    d_lru = conv_b.shape[-1]
    d_ff = w_ff_out.shape[1]
    heads_per_tile = MXU_TILE // lru_w_r.shape[-1]

    y_prompt, y_sample = x_prompt, x_sample
    lru_list, re_list, im_list = [], [], []
    for l in range(depth):
        cvecs = jnp.concatenate([c, c_ctx[None], jnp.zeros((SUBLANES - 1, d), F32)], axis=0)
        mod = _mod_call(cvecs, w_mod[l], b_mod[l])
        mod_dec = mod[0:SUBLANES]
        mod_ctx = jnp.broadcast_to(mod[SUBLANES:SUBLANES + 1], (SUBLANES, mod.shape[1]))

        p = {
            "g_pre_mix": g_pre_mix[l], "g_post_mix": g_post_mix[l], "g_pre_ffn": g_pre_ffn[l],
            "g_post_ffn": g_post_ffn[l], "w_in": w_in[l].astype(BF16), "conv_w": conv_w[l], "conv_b": conv_b[l],
            "wr_bd": [_block_diag_tiles(lru_w_r[l, dr], heads_per_tile).astype(BF16) for dr in range(n_dir)],
            "wi_bd": [_block_diag_tiles(lru_w_i[l, dr], heads_per_tile).astype(BF16) for dr in range(n_dir)],
            "lru_b_r": lru_b_r[l], "lru_b_i": lru_b_i[l],
            "lru_lambda": lru_lambda[l],
            "s5_ops": [_s5_chunk_operators(s5_a_re[l, dr], s5_a_im[l, dr], s5_log_dt[l, dr], s5_b_re[l, dr],
                                           s5_b_im[l, dr], s5_c_re[l, dr], s5_c_im[l, dr], reverse=bool(dr))
                       for dr in range(n_dir)],
            "s5_d": s5_d[l], "s5_w_glu": s5_w_glu[l].astype(BF16), "s5_b_glu": s5_b_glu[l],
            "w_proj_lru": w_proj_lru[l].astype(BF16), "w_proj_s5": w_proj_s5[l].astype(BF16),
            "w_gate": (0.5 * w_gate[l]).astype(BF16), "b_gate": b_gate[l], "w_out": (0.5 * w_out[l]).astype(BF16),
            "w_ff1": w_ff_in[l, :, :d_ff].astype(BF16), "w_ff3": w_ff_in[l, :, d_ff:].astype(BF16),
            "w_ff2": w_ff_out[l].astype(BF16),
        }
        ns = n_grp * n_st
        zero_lru = [jnp.zeros((g_ctx, SUBLANES, d_lru), F32)] * n_dir
        zero_s5 = [jnp.zeros((g_ctx, SUBLANES, ns), F32)] * n_dir
        y_prompt, f_lru, f_re, f_im = _layer(y_prompt, mod_ctx, p, zero_lru, zero_s5, zero_s5, colmajor=False)
        lru_list.append(jnp.stack([f.reshape(n_ctx, d_lru) for f in f_lru], axis=1))
        re_list.append(jnp.stack([f.reshape(n_ctx, n_grp, n_st) for f in f_re], axis=1))
        im_list.append(jnp.stack([f.reshape(n_ctx, n_grp, n_st) for f in f_im], axis=1))

        h0_lru = [state_lru[:, l, dr].astype(F32)[None] for dr in range(n_dir)]
        h0_re = [state_s5_re[:, l, dr].reshape(1, n_dec, ns) for dr in range(n_dir)]
        h0_im = [state_s5_im[:, l, dr].reshape(1, n_dec, ns) for dr in range(n_dir)]
        y_sample, _, _, _ = _layer(y_sample, mod_dec, p, h0_lru, h0_re, h0_im, colmajor=True)
    return (y_prompt, y_sample, jnp.stack(lru_list, axis=1), jnp.stack(re_list, axis=1),
            jnp.stack(im_list, axis=1))
```

```python
import functools

import jax
import jax.numpy as jnp
from jax import lax
from jax.experimental import pallas as pl
from jax.experimental.pallas import tpu as pltpu

F32 = jnp.float32
BF16 = jnp.bfloat16

EPS = 1e-6
LRU_C = 8.0
LOG2_E = 1.4426950408889634
GRID_W = 64
SUBLANES = 8
MXU_TILE = 256
S5_GROUP = 16
S5_STATE = 64
VMEM_LIMIT = 56 * 1024 * 1024

T_BLK = 64


def _const_spec(shape):
    nd = len(shape)
    return pl.BlockSpec(shape, lambda *_: (0,) * nd, pipeline_mode=pl.Buffered(1))


def _params(sem):
    return pltpu.CompilerParams(dimension_semantics=sem, vmem_limit_bytes=VMEM_LIMIT)


def _dot(a, b):
    return jnp.dot(a, b, preferred_element_type=F32)


def _rms(x):
    return x * lax.rsqrt(jnp.mean(x * x, axis=-1, keepdims=True) + EPS)


def _mod_kernel(c_ref, w_ref, b_ref, o_ref):
    c = c_ref[...]
    s = c * jax.nn.sigmoid(c)
    o_ref[...] = jnp.dot(s, w_ref[...], preferred_element_type=F32,
                         precision=lax.Precision.HIGHEST) + b_ref[...]


def _mod_call(cvecs, w_mod, b_mod):
    rows, d = cvecs.shape
    n = w_mod.shape[1]
    nb = 4
    return pl.pallas_call(
        _mod_kernel,
        grid=(nb,),
        in_specs=[pl.BlockSpec((rows, d), lambda j: (0, 0)),
                  pl.BlockSpec((d, n // nb), lambda j: (0, j)),
                  pl.BlockSpec((1, n // nb), lambda j: (0, j))],
        out_specs=pl.BlockSpec((rows, n // nb), lambda j: (0, j)),
        out_shape=jax.ShapeDtypeStruct((rows, n), F32),
        compiler_params=_params(("arbitrary",)),
        name="mod",
    )(cvecs, w_mod, b_mod.reshape(1, n))


def _stage1_kernel(x_ref, mod_ref, g_ref, w_ref, xtm_ref, xa_ref, ga_ref, xs_ref, *, colmajor):
    _, tt, d = x_ref.shape
    rows = tt * SUBLANES
    x = jnp.swapaxes(x_ref[...], 0, 1)
    sh1 = mod_ref[:, 0:d][None]
    sc1 = mod_ref[:, d:2 * d][None]
    hn = _rms(x) * g_ref[...] * (1.0 + sc1) + sh1
    xtm_ref[...] = x.reshape(rows, d)
    z = _dot(hn.reshape(rows, d).astype(BF16), w_ref[...])
    dl = xa_ref.shape[-1]
    xa_ref[...] = z[:, 0:dl].astype(BF16)
    ga_ref[...] = z[:, dl:2 * dl].astype(BF16)
    zs = z[:, 2 * dl:]
    if colmajor:
        xs_ref[:, 0] = zs.reshape(tt, SUBLANES, zs.shape[-1])
    else:
        xs_ref[...] = zs.reshape(tt, SUBLANES, zs.shape[-1])


def _stage1_call(x, mod8, g_pre, w_in, d_lru, d_s5, colmajor):
    b, t, d = x.shape
    g = b // SUBLANES
    tt = T_BLK
    nt = t // tt
    rows = tt * SUBLANES
    if colmajor:
        assert g == 1 and tt == GRID_W
        xs_shape = (GRID_W, nt, SUBLANES, d_s5)
        xs_spec = pl.BlockSpec((GRID_W, 1, SUBLANES, d_s5), lambda gi, ti: (0, ti, 0, 0))
    else:
        xs_shape = (g, t, SUBLANES, d_s5)
        xs_spec = pl.BlockSpec((None, tt, SUBLANES, d_s5), lambda gi, ti: (gi, ti, 0, 0))
    row_spec = lambda c: pl.BlockSpec((None, rows, c), lambda gi, ti: (gi, ti, 0))
    return pl.pallas_call(
        functools.partial(_stage1_kernel, colmajor=colmajor),
        grid=(g, nt),
        in_specs=[pl.BlockSpec((SUBLANES, tt, d), lambda gi, ti: (gi, ti, 0)),
                  _const_spec(mod8.shape), _const_spec((1, d)), _const_spec(w_in.shape)],
        out_specs=[row_spec(d), row_spec(d_lru), row_spec(d_lru), xs_spec],
        out_shape=[jax.ShapeDtypeStruct((g, t * SUBLANES, d), F32),
                   jax.ShapeDtypeStruct((g, t * SUBLANES, d_lru), BF16),
                   jax.ShapeDtypeStruct((g, t * SUBLANES, d_lru), BF16),
                   jax.ShapeDtypeStruct(xs_shape, F32)],
        compiler_params=_params(("arbitrary", "arbitrary")),
        name="stage1",
    )(x, mod8, g_pre.reshape(1, d), w_in)


def _lru_kernel(*refs, reverse, nt, from_conv):
    if from_conv:
        (xm_ref, xp_ref, xn_ref, cw_ref, cb_ref, wr_ref, wi_ref, br_ref, bi_ref, lam_ref, h0_ref,
         h_ref, fin_ref, hu_out_ref, xe_scr, a_scr, b_scr, carry) = refs
        rows, c = xm_ref.shape
    else:
        (hu_ref, wr_ref, wi_ref, br_ref, bi_ref, lam_ref, h0_ref, h_ref, fin_ref, a_scr, b_scr, carry) = refs
        rows, c = hu_ref.shape
    tt = rows // SUBLANES
    i = pl.program_id(1)
    blk = (nt - 1 - i) if reverse else i

    @pl.when(i == 0)
    def _():
        carry[...] = h0_ref[...]

    if from_conv:
        halo = 2 * SUBLANES
        prev_ok = (blk > 0).astype(F32)
        next_ok = (blk < nt - 1).astype(F32)
        xe_scr[0:halo] = xp_ref[...].astype(F32) * prev_ok
        xe_scr[halo:halo + rows] = xm_ref[...].astype(F32)
        xe_scr[halo + rows:halo + rows + SUBLANES] = xn_ref[0:SUBLANES].astype(F32) * next_ok

    for j in range(c // MXU_TILE):
        cs = slice(j * MXU_TILE, (j + 1) * MXU_TILE)
        cc = (-0.5 * LRU_C * LOG2_E) * jax.nn.softplus(-lam_ref[:, cs])
        if from_conv:
            hu = 0.5 * cb_ref[:, cs]
            for k in range(4):
                hu = hu + (0.5 * cw_ref[k:k + 1, cs]) * xe_scr[k * SUBLANES:k * SUBLANES + rows, cs]
            hub = hu.astype(BF16)
            hu_out_ref[:, cs] = hub
        else:
            hub = hu_ref[:, cs]
            hu = hub.astype(F32)
        tr = jnp.tanh(_dot(hub, wr_ref[j]) + 0.5 * br_ref[:, cs])
        ti = jnp.tanh(_dot(hub, wi_ref[j]) + 0.5 * bi_ref[:, cs])
        a = jnp.exp2(cc * tr + cc)
        y = 1.0 - a * a
        a_scr[:, cs] = a
        b_scr[:, cs] = jnp.where(y > 0.0, y * lax.rsqrt(y), 0.0) * (hu * ti + hu)

    def step(s, h):
        t = (tt - 1 - s) if reverse else s
        sl = pl.ds(pl.multiple_of(t * SUBLANES, SUBLANES), SUBLANES)
        h = a_scr[sl, :] * h + b_scr[sl, :]
        b_scr[sl, :] = h
        return h

    h = lax.fori_loop(0, tt, step, carry[...], unroll=8)
    carry[...] = h
    fin_ref[...] = h
    h_ref[...] = b_scr[...].astype(BF16)


def _lru_call(x, conv, wr_bd, wi_bd, b_r, b_i, lam, h0, reverse):
    g, n, c = x.shape
    tt = T_BLK
    rows = tt * SUBLANES
    nt = n // rows
    blk = (lambda ti: nt - 1 - ti) if reverse else (lambda ti: ti)
    main = pl.BlockSpec((None, rows, c), lambda gi, ti: (gi, blk(ti), 0))
    state = pl.BlockSpec((None, SUBLANES, c), lambda gi, ti: (gi, 0, 0))
    vec = lambda: _const_spec((1, c))
    gate_specs = [_const_spec(wr_bd.shape), _const_spec(wi_bd.shape), vec(), vec(), vec(), state]
    gate_args = (wr_bd, wi_bd, b_r.reshape(1, c), b_i.reshape(1, c), lam.reshape(1, c), h0)
    out_specs = [main, state]
    out_shape = [jax.ShapeDtypeStruct((g, n, c), BF16), jax.ShapeDtypeStruct((g, SUBLANES, c), F32)]
    scratch = [pltpu.VMEM((rows, c), F32), pltpu.VMEM((rows, c), F32), pltpu.VMEM((SUBLANES, c), F32)]
    if conv is not None:
        hb = 2 * SUBLANES
        per = rows // hb
        nhb = n // hb
        prev = pl.BlockSpec((None, hb, c), lambda gi, ti: (gi, jnp.maximum(blk(ti) * per - 1, 0), 0))
        nxt = pl.BlockSpec((None, hb, c), lambda gi, ti: (gi, jnp.minimum((blk(ti) + 1) * per, nhb - 1), 0))
        in_specs = [main, prev, nxt, _const_spec((4, c)), vec()] + gate_specs
        args = (x, x, x, conv[0], conv[1].reshape(1, c)) + gate_args
        out_specs = out_specs + [main]
        out_shape = out_shape + [jax.ShapeDtypeStruct((g, n, c), BF16)]
        scratch = [pltpu.VMEM((rows + 4 * SUBLANES, c), F32)] + scratch
    else:
        in_specs = [main] + gate_specs
        args = (x,) + gate_args
    return pl.pallas_call(
        functools.partial(_lru_kernel, reverse=reverse, nt=nt, from_conv=conv is not None),
        grid=(g, nt),
        in_specs=in_specs,
        out_specs=out_specs,
        out_shape=out_shape,
        scratch_shapes=scratch,
        compiler_params=_params(("arbitrary", "arbitrary")),
        name="lru_bwd" if reverse else "lru_fwd",
    )(*args)


S5_CHUNK = 16
S5_T_BLK = 512
S5_PAIR_UNROLL = 8


def _s5_kernel(xs_ref, m_ref, p_ref, q_ref, ar_ref, ai_ref, h0r_ref, h0i_ref, y_ref, finr_ref, fini_ref,
               xy_scr, st_scr, hin_scr, cr, ci, *, reverse):
    nseq, nchunk = xs_ref.shape[0], xs_ref.shape[1]
    ch = xs_ref.shape[-1]
    n = nseq * nchunk * SUBLANES
    ngrp = ch // S5_GROUP
    npair = ngrp // 2
    kdim = S5_CHUNK * S5_GROUP
    ps = S5_STATE
    unroll = S5_PAIR_UNROLL
    i = pl.program_id(1)

    @pl.when(i == 0)
    def _():
        for j in range(npair):
            for sq in range(nseq):
                cr[j, sq] = h0r_ref[sq, :, 2 * ps * j:2 * ps * (j + 1)]
                ci[j, sq] = h0i_ref[sq, :, 2 * ps * j:2 * ps * (j + 1)]

    for tau in range(S5_CHUNK):
        x = xs_ref[:, :, tau].reshape(n, ch).astype(BF16)
        xy_scr[:, tau] = x.T.reshape(ngrp, S5_GROUP, n)

    def pairs(jj, carry_):
        base = jj * unroll
        rs = [_dot(jnp.concatenate([m_ref[2 * base + e], p_ref[2 * base + e]], axis=0),
                   xy_scr[2 * base + e].reshape(kdim, n)) for e in range(2 * unroll)]
        for u in range(unroll):
            r0, r1 = rs[2 * u], rs[2 * u + 1]
            st = jnp.concatenate([r0[kdim:kdim + ps], r1[kdim:kdim + ps], r0[kdim + ps:], r1[kdim + ps:]], axis=0)
            st_scr[u] = st.astype(BF16).T.astype(F32)
        ars = [jnp.broadcast_to(ar_ref[base + u], (SUBLANES, 2 * ps)) for u in range(unroll)]
        ais = [jnp.broadcast_to(ai_ref[base + u], (SUBLANES, 2 * ps)) for u in range(unroll)]
        state = [[(cr[base + u, sq], ci[base + u, sq]) for sq in range(nseq)] for u in range(unroll)]
        for s in range(nchunk):
            m = (nchunk - 1 - s) if reverse else s
            for u in range(unroll):
                for sq in range(nseq):
                    row = (sq * nchunk + m) * SUBLANES
                    rows = slice(row, row + SUBLANES)
                    hr, hi = state[u][sq]
                    hin_scr[u, rows, 0:2 * ps] = hr
                    hin_scr[u, rows, 2 * ps:4 * ps] = hi
                    sr = st_scr[u, rows, 0:2 * ps]
                    si = st_scr[u, rows, 2 * ps:4 * ps]
                    state[u][sq] = (ars[u] * hr - ais[u] * hi + sr, ars[u] * hi + ais[u] * hr + si)
        for u in range(unroll):
            for sq in range(nseq):
                cr[base + u, sq], ci[base + u, sq] = state[u][sq]
        hts = [hin_scr[u].astype(BF16).T for u in range(unroll)]
        for u in range(unroll):
            ht = hts[u]
            for e in range(2):
                het = jnp.concatenate([ht[e * ps:(e + 1) * ps], ht[(2 + e) * ps:(3 + e) * ps]], axis=0)
                y = rs[2 * u + e][0:kdim] + _dot(q_ref[2 * (base + u) + e], het)
                xy_scr[2 * (base + u) + e] = y.astype(BF16).reshape(S5_CHUNK, S5_GROUP, n)
        return carry_

    lax.fori_loop(0, npair // unroll, pairs, 0)

    for j in range(npair):
        for sq in range(nseq):
            finr_ref[sq, :, 2 * ps * j:2 * ps * (j + 1)] = cr[j, sq]
            fini_ref[sq, :, 2 * ps * j:2 * ps * (j + 1)] = ci[j, sq]
    for tau in range(S5_CHUNK):
        yt = xy_scr[:, tau].reshape(ch, n)
        y_ref[:, :, tau] = yt.T.astype(F32).reshape(nseq, nchunk, SUBLANES, ch)


def _s5_call(xs, m, p, q, a16_re, a16_im, h0_re, h0_im, reverse):
    g, t, _, ch = xs.shape
    ngrp = ch // S5_GROUP
    ns = ngrp * S5_STATE
    tt = min(S5_T_BLK, t)
    nt = t // tt
    nseq = min(g, S5_T_BLK // tt)
    nchunk = tt // S5_CHUNK
    n = nseq * nchunk * SUBLANES
    xs5 = xs.reshape(g, t // S5_CHUNK, S5_CHUNK, SUBLANES, ch)
    blk = (lambda ti: nt - 1 - ti) if reverse else (lambda ti: ti)
    main = pl.BlockSpec((nseq, nchunk, S5_CHUNK, SUBLANES, ch), lambda gi, ti: (gi, blk(ti), 0, 0, 0))
    st = pl.BlockSpec((nseq, SUBLANES, ns), lambda gi, ti: (gi, 0, 0))
    y, fr, fi = pl.pallas_call(
        functools.partial(_s5_kernel, reverse=reverse),
        grid=(g // nseq, nt),
        in_specs=[main, _const_spec(m.shape), _const_spec(p.shape), _const_spec(q.shape), _const_spec(a16_re.shape),
                  _const_spec(a16_im.shape), st, st],
        out_specs=[main, st, st],
        out_shape=[jax.ShapeDtypeStruct(xs5.shape, F32), jax.ShapeDtypeStruct((g, SUBLANES, ns), F32),
                   jax.ShapeDtypeStruct((g, SUBLANES, ns), F32)],
        scratch_shapes=[pltpu.VMEM((ngrp, S5_CHUNK, S5_GROUP, n), BF16),
                        pltpu.VMEM((S5_PAIR_UNROLL, n, 4 * S5_STATE), F32),
                        pltpu.VMEM((S5_PAIR_UNROLL, n, 4 * S5_STATE), F32),
                        pltpu.VMEM((ngrp // 2, nseq, SUBLANES, 2 * S5_STATE), F32),
                        pltpu.VMEM((ngrp // 2, nseq, SUBLANES, 2 * S5_STATE), F32)],
        compiler_params=_params(("arbitrary", "arbitrary")),
        name="s5_bwd" if reverse else "s5_fwd",
    )(xs5, m, p, q, a16_re, a16_im, h0_re, h0_im)
    return y.reshape(xs.shape), fr, fi


def _s5_chunk_operators(a_re, a_im, log_dt, b_re, b_im, c_re, c_im, reverse):
    hi_p = lax.Precision.HIGHEST
    L = S5_CHUNK
    ngrp, ps, hh = b_re.shape
    dt = jnp.exp(log_dt)[:, None]
    lr, li = a_re * dt, a_im * dt

    def powers(e, xr, xi):
        mag = jnp.exp(e * xr)
        return mag * jnp.cos(e * xi), mag * jnp.sin(e * xi)

    abar_re, abar_im = powers(1.0, lr, li)
    den = a_re * a_re + a_im * a_im
    nr, ni = abar_re - 1.0, abar_im
    f_re = (nr * a_re + ni * a_im) / den
    f_im = (ni * a_re - nr * a_im) / den
    bb_re = f_re[..., None] * b_re - f_im[..., None] * b_im
    bb_im = f_re[..., None] * b_im + f_im[..., None] * b_re

    steps = jnp.arange(L, dtype=F32)
    pk_re, pk_im = powers(steps[None, :, None], lr[:, None, :], li[:, None, :])
    d_re = pk_re[..., None] * bb_re[:, None] - pk_im[..., None] * bb_im[:, None]
    d_im = pk_re[..., None] * bb_im[:, None] + pk_im[..., None] * bb_re[:, None]
    kk = (jnp.einsum("gip,gkpj->gikj", c_re, d_re, precision=hi_p)
          - jnp.einsum("gip,gkpj->gikj", c_im, d_im, precision=hi_p)).astype(BF16)
    pad = jnp.zeros((ngrp, hh, L - 1, hh), BF16)
    lags = jnp.concatenate([pad, kk], axis=2) if reverse else jnp.concatenate([kk[:, :, ::-1], pad], axis=2)
    lags = lags.reshape(ngrp, hh, (2 * L - 1) * hh)
    m = jnp.stack([lags[:, :, (L - 1 - t) * hh:(2 * L - 1 - t) * hh] for t in range(L)], axis=1)
    m = m.reshape(ngrp, L * hh, L * hh)

    lr2 = jnp.concatenate([lr, lr], axis=-1)
    li2 = jnp.concatenate([li, li], axis=-1)
    e_p = steps if reverse else (L - 1.0 - steps)
    pr2, pi2 = powers(e_p[None, None, :], lr2[:, :, None], li2[:, :, None])
    pa = jnp.concatenate([bb_re, bb_im], axis=1)
    pb = jnp.concatenate([-bb_im, bb_re], axis=1)
    p = (pr2[..., None] * pa[:, :, None, :] + pi2[..., None] * pb[:, :, None, :]).reshape(ngrp, 2 * ps, L * hh)

    e_q = (L - steps) if reverse else (steps + 1.0)
    qr2, qi2 = powers(e_q[None, :, None], lr2[:, None, :], li2[:, None, :])
    qa = jnp.concatenate([c_re, -c_im], axis=-1)
    qb = jnp.concatenate([-c_im, -c_re], axis=-1)
    q = (qa[:, None] * qr2[:, :, None, :] + qb[:, None] * qi2[:, :, None, :]).reshape(ngrp, L * hh, 2 * ps)

    a16_re, a16_im = powers(float(L), lr, li)
    return (m.astype(BF16), p.astype(BF16), q.astype(BF16), a16_re.reshape(ngrp // 2, 1, 2 * ps),
            a16_im.reshape(ngrp // 2, 1, 2 * ps))


def _mix_kernel(x_ref, ga_ref, hf_ref, hb_ref, yf_ref, yb_ref, xs_ref, mod_ref, gpre_ref, gpost_ref,
                wg_ref, bg_ref, wpl_ref, wps_ref, wo_ref, d_ref, wglu_ref, bglu_ref, o_ref, *, colmajor):
    rows, d = x_ref.shape
    tt = rows // SUBLANES
    x3 = x_ref[...].reshape(tt, SUBLANES, d)
    sh1 = mod_ref[:, 0:d][None]
    sc1 = mod_ref[:, d:2 * d][None]
    g1 = mod_ref[:, 2 * d:3 * d][None]
    hn = (_rms(x3) * gpre_ref[...] * (1.0 + sc1) + sh1).reshape(rows, d).astype(BF16)
    zg = _dot(hn, wg_ref[...])

    ha = hf_ref[...].astype(F32) + hb_ref[...].astype(F32)
    ya = jax.nn.gelu(ga_ref[...].astype(F32)) * ha
    pa = _dot(ya.astype(BF16), wpl_ref[...])

    def s5_rows(ref):
        v = ref[:, 0] if colmajor else ref[...]
        return v.reshape(rows, v.shape[-1])

    us = s5_rows(xs_ref)
    hs = s5_rows(yf_ref) + s5_rows(yb_ref)
    vs = jax.nn.gelu(hs + d_ref[...] * us)
    zglu = _dot(vs.astype(BF16), wglu_ref[...])
    tg = jnp.tanh(zg + bg_ref[...])
    ys = vs * jax.nn.sigmoid(zglu + bglu_ref[...])
    pb = _dot(ys.astype(BF16), wps_ref[...])

    mm = (pa + tg[:, 0:d] * pa) + (pb + tg[:, d:2 * d] * pb)
    m = _dot(mm.astype(BF16), wo_ref[...])
    mn = (_rms(m) * gpost_ref[...]).reshape(tt, SUBLANES, d)
    o_ref[...] = (x3 + g1 * mn).reshape(rows, d)


def _mix_call(xtm, ga, hf, hb, yf, yb, xs, mod8, g_pre, g_post, w_gate, b_gate, w_pl, w_ps, w_out,
              s5_d, w_glu, b_glu, colmajor):
    g, n, d = xtm.shape
    tt = T_BLK
    rows = tt * SUBLANES
    nt = n // rows
    ch = xs.shape[-1]
    row_spec = lambda c: pl.BlockSpec((None, rows, c), lambda gi, ti: (gi, ti, 0))
    if colmajor:
        s5_spec = pl.BlockSpec((GRID_W, 1, SUBLANES, ch), lambda gi, ti: (0, ti, 0, 0))
    else:
        s5_spec = pl.BlockSpec((None, tt, SUBLANES, ch), lambda gi, ti: (gi, ti, 0, 0))
    dl = ga.shape[-1]
    consts = [mod8, g_pre.reshape(1, d), g_post.reshape(1, d), w_gate, 0.5 * b_gate.reshape(1, 2 * d), w_pl, w_ps,
              w_out, s5_d.reshape(1, ch), w_glu, b_glu.reshape(1, ch)]
    return pl.pallas_call(
        functools.partial(_mix_kernel, colmajor=colmajor),
        grid=(g, nt),
        in_specs=[row_spec(d), row_spec(dl), row_spec(dl), row_spec(dl), s5_spec, s5_spec, s5_spec]
                 + [_const_spec(a.shape) for a in consts],
        out_specs=row_spec(d),
        out_shape=jax.ShapeDtypeStruct((g, n, d), F32),
        compiler_params=_params(("arbitrary", "arbitrary")),
        name="mix",
    )(xtm, ga, hf, hb, yf, yb, xs, *consts)


FF_CHUNKS = 11


def _ffn_kernel(x_ref, mod_ref, gpre_ref, gpost_ref, w1_ref, w3_ref, w2_ref, o_ref):
    rows, d = x_ref.shape
    tt = rows // SUBLANES
    x3 = x_ref[...].reshape(tt, SUBLANES, d)
    sh2 = mod_ref[:, 3 * d:4 * d][None]
    sc2 = mod_ref[:, 4 * d:5 * d][None]
    g2 = mod_ref[:, 5 * d:6 * d][None]
    hn = (_rms(x3) * gpre_ref[...] * (1.0 + sc2) + sh2).reshape(rows, d).astype(BF16)
    dff = w1_ref.shape[-1]
    cw = dff // FF_CHUNKS
    f = jnp.zeros((rows, d), F32)
    for k in range(FF_CHUNKS):
        cs = slice(k * cw, (k + 1) * cw)
        u1 = _dot(hn, w1_ref[:, cs])
        u3 = _dot(hn, w3_ref[:, cs])
        act = (u1 * jax.nn.sigmoid(u1) * u3).astype(BF16)
        f = f + _dot(act, w2_ref[cs, :])
    fn = (_rms(f) * gpost_ref[...]).reshape(tt, SUBLANES, d)
    o_ref[...] = jnp.swapaxes(x3 + g2 * fn, 0, 1)


def _ffn_call(x1, mod8, g_pre, g_post, w1, w3, w2):
    g, n, d = x1.shape
    tt = T_BLK
    rows = tt * SUBLANES
    nt = n // rows
    consts = [mod8, g_pre.reshape(1, d), g_post.reshape(1, d), w1, w3, w2]
    return pl.pallas_call(
        _ffn_kernel,
        grid=(g, nt),
        in_specs=[pl.BlockSpec((None, rows, d), lambda gi, ti: (gi, ti, 0))]
                 + [_const_spec(a.shape) for a in consts],
        out_specs=pl.BlockSpec((SUBLANES, tt, d), lambda gi, ti: (gi, ti, 0)),
        out_shape=jax.ShapeDtypeStruct((g * SUBLANES, nt * tt, d), F32),
        compiler_params=_params(("arbitrary", "arbitrary")),
        name="ffn",
    )(x1, *consts)


def _block_diag_tiles(w, per_tile):
    nb, k, m = w.shape
    w = w.reshape(nb // per_tile, per_tile, k, m)
    eye = jnp.eye(per_tile, dtype=w.dtype)
    t = jnp.einsum("tbkm,bc->tbkcm", w, eye)
    return t.reshape(nb // per_tile, per_tile * k, per_tile * m)


def _layer(x, mod8, p, h0_lru, h0_re, h0_im, colmajor):
    b, t, d = x.shape
    d_lru = p["conv_b"].shape[-1]
    d_s5 = p["s5_d"].shape[-1]
    xtm, xa, ga, xs = _stage1_call(x, mod8, p["g_pre_mix"], p["w_in"], d_lru, d_s5, colmajor)
    if colmajor:
        xs_seq = xs.reshape(1, GRID_W * xs.shape[1], SUBLANES, d_s5)
    else:
        xs_seq = xs
    hs, fins = [], []
    ys, fr, fi = [], [], []
    hu = None
    for dr in range(2):
        res = _lru_call(xa if dr == 0 else hu, (p["conv_w"], p["conv_b"]) if dr == 0 else None, p["wr_bd"][dr],
                        p["wi_bd"][dr], p["lru_b_r"][dr], p["lru_b_i"][dr], p["lru_lambda"][dr], h0_lru[dr],
                        reverse=bool(dr))
        if dr == 0:
            hu = res[2]
        hs.append(res[0])
        fins.append(res[1])
        y, r_, i_ = _s5_call(xs_seq, *p["s5_ops"][dr], h0_re[dr], h0_im[dr], reverse=bool(dr))
        ys.append(y.reshape(xs.shape))
        fr.append(r_)
        fi.append(i_)
    x1 = _mix_call(xtm, ga, hs[0], hs[1], ys[0], ys[1], xs, mod8, p["g_pre_mix"], p["g_post_mix"], p["w_gate"],
                   p["b_gate"], p["w_proj_lru"], p["w_proj_s5"], p["w_out"], p["s5_d"], p["s5_w_glu"],
                   p["s5_b_glu"], colmajor)
    out = _ffn_call(x1, mod8, p["g_pre_ffn"], p["g_post_ffn"], p["w_ff1"], p["w_ff3"], p["w_ff2"])
    return out, fins, fr, fi


def kernel(x_prompt, x_sample, c, state_lru, state_s5_re, state_s5_im, c_ctx, w_mod, b_mod, g_pre_mix, g_post_mix,
           g_pre_ffn, g_post_ffn, w_in, conv_w, conv_b, lru_w_r, lru_b_r, lru_w_i, lru_b_i, lru_lambda,
           s5_a_re, s5_a_im, s5_log_dt, s5_b_re, s5_b_im, s5_c_re, s5_c_im, s5_d, s5_w_glu, s5_b_glu,
           w_proj_lru, w_proj_s5, w_gate, b_gate, w_out, w_ff_in, w_ff_out):
    depth = w_mod.shape[0]
    n_ctx, _, d = x_prompt.shape
    n_dec = x_sample.shape[0]
    assert n_dec == SUBLANES and n_ctx % SUBLANES == 0
    g_ctx = n_ctx // SUBLANES
    n_dir, n_grp, n_st = s5_a_re.shape[1:]
    d_lru = conv_b.shape[-1]
    d_ff = w_ff_out.shape[1]
    heads_per_tile = MXU_TILE // lru_w_r.shape[-1]

    y_prompt, y_sample = x_prompt, x_sample
    lru_list, re_list, im_list = [], [], []
    for l in range(depth):
        cvecs = jnp.concatenate([c, c_ctx[None], jnp.zeros((SUBLANES - 1, d), F32)], axis=0)
        mod = _mod_call(cvecs, w_mod[l], b_mod[l])
        mod_dec = mod[0:SUBLANES]
        mod_ctx = jnp.broadcast_to(mod[SUBLANES:SUBLANES + 1], (SUBLANES, mod.shape[1]))

        p = {
            "g_pre_mix": g_pre_mix[l], "g_post_mix": g_post_mix[l], "g_pre_ffn": g_pre_ffn[l],
            "g_post_ffn": g_post_ffn[l], "w_in": w_in[l].astype(BF16), "conv_w": conv_w[l], "conv_b": conv_b[l],
            "wr_bd": [_block_diag_tiles(lru_w_r[l, dr], heads_per_tile).astype(BF16) for dr in range(n_dir)],
            "wi_bd": [_block_diag_tiles(lru_w_i[l, dr], heads_per_tile).astype(BF16) for dr in range(n_dir)],
            "lru_b_r": lru_b_r[l], "lru_b_i": lru_b_i[l],
            "lru_lambda": lru_lambda[l],
            "s5_ops": [_s5_chunk_operators(s5_a_re[l, dr], s5_a_im[l, dr], s5_log_dt[l, dr], s5_b_re[l, dr],
                                           s5_b_im[l, dr], s5_c_re[l, dr], s5_c_im[l, dr], reverse=bool(dr))
                       for dr in range(n_dir)],
            "s5_d": s5_d[l], "s5_w_glu": s5_w_glu[l].astype(BF16), "s5_b_glu": s5_b_glu[l],
            "w_proj_lru": w_proj_lru[l].astype(BF16), "w_proj_s5": w_proj_s5[l].astype(BF16),
            "w_gate": (0.5 * w_gate[l]).astype(BF16), "b_gate": b_gate[l], "w_out": (0.5 * w_out[l]).astype(BF16),
            "w_ff1": w_ff_in[l, :, :d_ff].astype(BF16), "w_ff3": w_ff_in[l, :, d_ff:].astype(BF16),
            "w_ff2": w_ff_out[l].astype(BF16),
        }
        ns = n_grp * n_st
        zero_lru = [jnp.zeros((g_ctx, SUBLANES, d_lru), F32)] * n_dir
        zero_s5 = [jnp.zeros((g_ctx, SUBLANES, ns), F32)] * n_dir
        y_prompt, f_lru, f_re, f_im = _layer(y_prompt, mod_ctx, p, zero_lru, zero_s5, zero_s5, colmajor=False)
        lru_list.append(jnp.stack([f.reshape(n_ctx, d_lru) for f in f_lru], axis=1))
        re_list.append(jnp.stack([f.reshape(n_ctx, n_grp, n_st) for f in f_re], axis=1))
        im_list.append(jnp.stack([f.reshape(n_ctx, n_grp, n_st) for f in f_im], axis=1))

        h0_lru = [state_lru[:, l, dr].astype(F32)[None] for dr in range(n_dir)]
        h0_re = [state_s5_re[:, l, dr].reshape(1, n_dec, ns) for dr in range(n_dir)]
        h0_im = [state_s5_im[:, l, dr].reshape(1, n_dec, ns) for dr in range(n_dir)]
        y_sample, _, _, _ = _layer(y_sample, mod_dec, p, h0_lru, h0_re, h0_im, colmajor=True)
    return (y_prompt, y_sample, jnp.stack(lru_list, axis=1), jnp.stack(re_list, axis=1),
            jnp.stack(im_list, axis=1))
```

```python
import functools

import jax
import jax.numpy as jnp
from jax import lax
from jax.experimental import pallas as pl
from jax.experimental.pallas import tpu as pltpu

F32 = jnp.float32
BF16 = jnp.bfloat16

EPS = 1e-6
LRU_C = 8.0
LOG2_E = 1.4426950408889634
GRID_W = 64
SUBLANES = 8
MXU_TILE = 256
S5_GROUP = 16
S5_STATE = 64
VMEM_LIMIT = 56 * 1024 * 1024

T_BLK = 64
LRU_T_BLK = 128


def _const_spec(shape):
    nd = len(shape)
    return pl.BlockSpec(shape, lambda *_: (0,) * nd, pipeline_mode=pl.Buffered(1))


def _params(sem):
    return pltpu.CompilerParams(dimension_semantics=sem, vmem_limit_bytes=VMEM_LIMIT)


def _dot(a, b):
    return jnp.dot(a, b, preferred_element_type=F32)


def _rms(x):
    return x * lax.rsqrt(jnp.mean(x * x, axis=-1, keepdims=True) + EPS)


def _mod_kernel(c_ref, w_ref, b_ref, o_ref):
    c = c_ref[...]
    s = c * jax.nn.sigmoid(c)
    o_ref[...] = jnp.dot(s, w_ref[...], preferred_element_type=F32,
                         precision=lax.Precision.HIGHEST) + b_ref[...]


def _mod_call(cvecs, w_mod, b_mod):
    rows, d = cvecs.shape
    n = w_mod.shape[1]
    nb = 4
    return pl.pallas_call(
        _mod_kernel,
        grid=(nb,),
        in_specs=[pl.BlockSpec((rows, d), lambda j: (0, 0)),
                  pl.BlockSpec((d, n // nb), lambda j: (0, j)),
                  pl.BlockSpec((1, n // nb), lambda j: (0, j))],
        out_specs=pl.BlockSpec((rows, n // nb), lambda j: (0, j)),
        out_shape=jax.ShapeDtypeStruct((rows, n), F32),
        compiler_params=_params(("arbitrary",)),
        name="mod",
    )(cvecs, w_mod, b_mod.reshape(1, n))


def _stage1_kernel(x_ref, mod_ref, g_ref, w_ref, xtm_ref, xa_ref, ga_ref, xs_ref, *, colmajor):
    _, tt, d = x_ref.shape
    rows = tt * SUBLANES
    x = jnp.swapaxes(x_ref[...], 0, 1)
    sh1 = mod_ref[:, 0:d][None]
    sc1 = mod_ref[:, d:2 * d][None]
    hn = _rms(x) * g_ref[...] * (1.0 + sc1) + sh1
    xtm_ref[...] = x.reshape(rows, d)
    z = _dot(hn.reshape(rows, d).astype(BF16), w_ref[...])
    dl = xa_ref.shape[-1]
    xa_ref[...] = z[:, 0:dl].astype(BF16)
    ga_ref[...] = z[:, dl:2 * dl].astype(BF16)
    zs = z[:, 2 * dl:]
    if colmajor:
        xs_ref[:, 0] = zs.reshape(tt, SUBLANES, zs.shape[-1])
    else:
        xs_ref[...] = zs.reshape(tt, SUBLANES, zs.shape[-1])


def _stage1_call(x, mod8, g_pre, w_in, d_lru, d_s5, colmajor):
    b, t, d = x.shape
    g = b // SUBLANES
    tt = T_BLK
    nt = t // tt
    rows = tt * SUBLANES
    if colmajor:
        assert g == 1 and tt == GRID_W
        xs_shape = (GRID_W, nt, SUBLANES, d_s5)
        xs_spec = pl.BlockSpec((GRID_W, 1, SUBLANES, d_s5), lambda gi, ti: (0, ti, 0, 0))
    else:
        xs_shape = (g, t, SUBLANES, d_s5)
        xs_spec = pl.BlockSpec((None, tt, SUBLANES, d_s5), lambda gi, ti: (gi, ti, 0, 0))
    row_spec = lambda c: pl.BlockSpec((None, rows, c), lambda gi, ti: (gi, ti, 0))
    return pl.pallas_call(
        functools.partial(_stage1_kernel, colmajor=colmajor),
        grid=(g, nt),
        in_specs=[pl.BlockSpec((SUBLANES, tt, d), lambda gi, ti: (gi, ti, 0)),
                  _const_spec(mod8.shape), _const_spec((1, d)), _const_spec(w_in.shape)],
        out_specs=[row_spec(d), row_spec(d_lru), row_spec(d_lru), xs_spec],
        out_shape=[jax.ShapeDtypeStruct((g, t * SUBLANES, d), F32),
                   jax.ShapeDtypeStruct((g, t * SUBLANES, d_lru), BF16),
                   jax.ShapeDtypeStruct((g, t * SUBLANES, d_lru), BF16),
                   jax.ShapeDtypeStruct(xs_shape, F32)],
        compiler_params=_params(("arbitrary", "arbitrary")),
        name="stage1",
    )(x, mod8, g_pre.reshape(1, d), w_in)


def _lru_kernel(*refs, reverse, nt, from_conv):
    if from_conv:
        (xm_ref, xp_ref, xn_ref, cw_ref, cb_ref, wr_ref, wi_ref, br_ref, bi_ref, lam_ref, h0_ref,
         h_ref, fin_ref, hu_out_ref, xe_scr, a_scr, b_scr, carry) = refs
        rows, c = xm_ref.shape
    else:
        (hu_ref, wr_ref, wi_ref, br_ref, bi_ref, lam_ref, h0_ref, h_ref, fin_ref, a_scr, b_scr, carry) = refs
        rows, c = hu_ref.shape
    tt = rows // SUBLANES
    i = pl.program_id(1)
    blk = (nt - 1 - i) if reverse else i

    @pl.when(i == 0)
    def _():
        carry[...] = h0_ref[...]

    if from_conv:
        halo = 2 * SUBLANES
        prev_ok = (blk > 0).astype(F32)
        next_ok = (blk < nt - 1).astype(F32)
        xe_scr[0:halo] = xp_ref[...].astype(F32) * prev_ok
        xe_scr[halo:halo + rows] = xm_ref[...].astype(F32)
        xe_scr[halo + rows:halo + rows + SUBLANES] = xn_ref[0:SUBLANES].astype(F32) * next_ok

    for j in range(c // MXU_TILE):
        cs = slice(j * MXU_TILE, (j + 1) * MXU_TILE)
        cc = (-0.5 * LRU_C * LOG2_E) * jax.nn.softplus(-lam_ref[:, cs])
        if from_conv:
            hu = 0.5 * cb_ref[:, cs]
            for k in range(4):
                hu = hu + (0.5 * cw_ref[k:k + 1, cs]) * xe_scr[k * SUBLANES:k * SUBLANES + rows, cs]
            hub = hu.astype(BF16)
            hu_out_ref[:, cs] = hub
        else:
            hub = hu_ref[:, cs]
            hu = hub.astype(F32)
        tr = jnp.tanh(_dot(hub, wr_ref[j]) + 0.5 * br_ref[:, cs])
        ti = jnp.tanh(_dot(hub, wi_ref[j]) + 0.5 * bi_ref[:, cs])
        a = jnp.exp2(cc * tr + cc)
        y = 1.0 - a * a
        a_scr[:, cs] = a
        b_scr[:, cs] = jnp.where(y > 0.0, y * lax.rsqrt(y), 0.0) * (hu * ti + hu)

    def step(s, h):
        t = (tt - 1 - s) if reverse else s
        sl = pl.ds(pl.multiple_of(t * SUBLANES, SUBLANES), SUBLANES)
        h = a_scr[sl, :] * h + b_scr[sl, :]
        b_scr[sl, :] = h
        return h

    h = lax.fori_loop(0, tt, step, carry[...], unroll=8)
    carry[...] = h
    fin_ref[...] = h
    h_ref[...] = b_scr[...].astype(BF16)


def _lru_call(x, conv, wr_bd, wi_bd, b_r, b_i, lam, h0, reverse):
    g, n, c = x.shape
    tt = LRU_T_BLK
    rows = tt * SUBLANES
    nt = n // rows
    blk = (lambda ti: nt - 1 - ti) if reverse else (lambda ti: ti)
    main = pl.BlockSpec((None, rows, c), lambda gi, ti: (gi, blk(ti), 0))
    state = pl.BlockSpec((None, SUBLANES, c), lambda gi, ti: (gi, 0, 0))
    vec = lambda: _const_spec((1, c))
    gate_specs = [_const_spec(wr_bd.shape), _const_spec(wi_bd.shape), vec(), vec(), vec(), state]
    gate_args = (wr_bd, wi_bd, b_r.reshape(1, c), b_i.reshape(1, c), lam.reshape(1, c), h0)
    out_specs = [main, state]
    out_shape = [jax.ShapeDtypeStruct((g, n, c), BF16), jax.ShapeDtypeStruct((g, SUBLANES, c), F32)]
    scratch = [pltpu.VMEM((rows, c), F32), pltpu.VMEM((rows, c), F32), pltpu.VMEM((SUBLANES, c), F32)]
    if conv is not None:
        hb = 2 * SUBLANES
        per = rows // hb
        nhb = n // hb
        prev = pl.BlockSpec((None, hb, c), lambda gi, ti: (gi, jnp.maximum(blk(ti) * per - 1, 0), 0))
        nxt = pl.BlockSpec((None, hb, c), lambda gi, ti: (gi, jnp.minimum((blk(ti) + 1) * per, nhb - 1), 0))
        in_specs = [main, prev, nxt, _const_spec((4, c)), vec()] + gate_specs
        args = (x, x, x, conv[0], conv[1].reshape(1, c)) + gate_args
        out_specs = out_specs + [main]
        out_shape = out_shape + [jax.ShapeDtypeStruct((g, n, c), BF16)]
        scratch = [pltpu.VMEM((rows + 4 * SUBLANES, c), F32)] + scratch
    else:
        in_specs = [main] + gate_specs
        args = (x,) + gate_args
    return pl.pallas_call(
        functools.partial(_lru_kernel, reverse=reverse, nt=nt, from_conv=conv is not None),
        grid=(g, nt),
        in_specs=in_specs,
        out_specs=out_specs,
        out_shape=out_shape,
        scratch_shapes=scratch,
        compiler_params=_params(("arbitrary", "arbitrary")),
        name="lru_bwd" if reverse else "lru_fwd",
    )(*args)


S5_CHUNK = 16
S5_T_BLK = 512
S5_PAIR_UNROLL = 16


def _s5_kernel(xs_ref, m_ref, p_ref, q_ref, ar_ref, ai_ref, h0r_ref, h0i_ref, y_ref, finr_ref, fini_ref,
               xy_scr, st_scr, hin_scr, cr, ci, *, reverse):
    nseq, nchunk = xs_ref.shape[0], xs_ref.shape[1]
    ch = xs_ref.shape[-1]
    n = nseq * nchunk * SUBLANES
    ngrp = ch // S5_GROUP
    npair = ngrp // 2
    kdim = S5_CHUNK * S5_GROUP
    ps = S5_STATE
    unroll = S5_PAIR_UNROLL
    i = pl.program_id(1)

    @pl.when(i == 0)
    def _():
        for j in range(npair):
            for sq in range(nseq):
                cr[j, sq] = h0r_ref[sq, :, 2 * ps * j:2 * ps * (j + 1)]
                ci[j, sq] = h0i_ref[sq, :, 2 * ps * j:2 * ps * (j + 1)]

    for tau in range(S5_CHUNK):
        x = xs_ref[:, :, tau].reshape(n, ch).astype(BF16)
        xy_scr[:, tau] = x.T.reshape(ngrp, S5_GROUP, n)

    def pairs(jj, carry_):
        base = jj * unroll
        rs = [_dot(jnp.concatenate([m_ref[2 * base + e], p_ref[2 * base + e]], axis=0),
                   xy_scr[2 * base + e].reshape(kdim, n)) for e in range(2 * unroll)]
        for u in range(unroll):
            r0, r1 = rs[2 * u], rs[2 * u + 1]
            st = jnp.concatenate([r0[kdim:kdim + ps], r1[kdim:kdim + ps], r0[kdim + ps:], r1[kdim + ps:]], axis=0)
            st_scr[u] = st.astype(BF16).T.astype(F32)
        ars = [jnp.broadcast_to(ar_ref[base + u], (SUBLANES, 2 * ps)) for u in range(unroll)]
        ais = [jnp.broadcast_to(ai_ref[base + u], (SUBLANES, 2 * ps)) for u in range(unroll)]
        state = [[(cr[base + u, sq], ci[base + u, sq]) for sq in range(nseq)] for u in range(unroll)]
        for s in range(nchunk):
            m = (nchunk - 1 - s) if reverse else s
            for u in range(unroll):
                for sq in range(nseq):
                    row = (sq * nchunk + m) * SUBLANES
                    rows = slice(row, row + SUBLANES)
                    hr, hi = state[u][sq]
                    hin_scr[u, rows, 0:2 * ps] = hr
                    hin_scr[u, rows, 2 * ps:4 * ps] = hi
                    sr = st_scr[u, rows, 0:2 * ps]
                    si = st_scr[u, rows, 2 * ps:4 * ps]
                    state[u][sq] = (ars[u] * hr - ais[u] * hi + sr, ars[u] * hi + ais[u] * hr + si)
        for u in range(unroll):
            for sq in range(nseq):
                cr[base + u, sq], ci[base + u, sq] = state[u][sq]
        hts = [hin_scr[u].astype(BF16).T for u in range(unroll)]
        for u in range(unroll):
            ht = hts[u]
            for e in range(2):
                het = jnp.concatenate([ht[e * ps:(e + 1) * ps], ht[(2 + e) * ps:(3 + e) * ps]], axis=0)
                y = rs[2 * u + e][0:kdim] + _dot(q_ref[2 * (base + u) + e], het)
                xy_scr[2 * (base + u) + e] = y.astype(BF16).reshape(S5_CHUNK, S5_GROUP, n)
        return carry_

    lax.fori_loop(0, npair // unroll, pairs, 0)

    for j in range(npair):
        for sq in range(nseq):
            finr_ref[sq, :, 2 * ps * j:2 * ps * (j + 1)] = cr[j, sq]
            fini_ref[sq, :, 2 * ps * j:2 * ps * (j + 1)] = ci[j, sq]
    for tau in range(S5_CHUNK):
        yt = xy_scr[:, tau].reshape(ch, n)
        y_ref[:, :, tau] = yt.T.astype(F32).reshape(nseq, nchunk, SUBLANES, ch)


def _s5_call(xs, m, p, q, a16_re, a16_im, h0_re, h0_im, reverse):
    g, t, _, ch = xs.shape
    ngrp = ch // S5_GROUP
    ns = ngrp * S5_STATE
    tt = min(S5_T_BLK, t)
    nt = t // tt
    nseq = min(g, S5_T_BLK // tt)
    nchunk = tt // S5_CHUNK
    n = nseq * nchunk * SUBLANES
    xs5 = xs.reshape(g, t // S5_CHUNK, S5_CHUNK, SUBLANES, ch)
    blk = (lambda ti: nt - 1 - ti) if reverse else (lambda ti: ti)
    main = pl.BlockSpec((nseq, nchunk, S5_CHUNK, SUBLANES, ch), lambda gi, ti: (gi, blk(ti), 0, 0, 0))
    st = pl.BlockSpec((nseq, SUBLANES, ns), lambda gi, ti: (gi, 0, 0))
    y, fr, fi = pl.pallas_call(
        functools.partial(_s5_kernel, reverse=reverse),
        grid=(g // nseq, nt),
        in_specs=[main, _const_spec(m.shape), _const_spec(p.shape), _const_spec(q.shape), _const_spec(a16_re.shape),
                  _const_spec(a16_im.shape), st, st],
        out_specs=[main, st, st],
        out_shape=[jax.ShapeDtypeStruct(xs5.shape, F32), jax.ShapeDtypeStruct((g, SUBLANES, ns), F32),
                   jax.ShapeDtypeStruct((g, SUBLANES, ns), F32)],
        scratch_shapes=[pltpu.VMEM((ngrp, S5_CHUNK, S5_GROUP, n), BF16),
                        pltpu.VMEM((S5_PAIR_UNROLL, n, 4 * S5_STATE), F32),
                        pltpu.VMEM((S5_PAIR_UNROLL, n, 4 * S5_STATE), F32),
                        pltpu.VMEM((ngrp // 2, nseq, SUBLANES, 2 * S5_STATE), F32),
                        pltpu.VMEM((ngrp // 2, nseq, SUBLANES, 2 * S5_STATE), F32)],
        compiler_params=_params(("arbitrary", "arbitrary")),
        name="s5_bwd" if reverse else "s5_fwd",
    )(xs5, m, p, q, a16_re, a16_im, h0_re, h0_im)
    return y.reshape(xs.shape), fr, fi


def _toeplitz_kernel(lag_ref, m_ref):
    ng, hh, width = lag_ref.shape
    n = S5_CHUNK * hh
    for g in range(ng):
        x = lag_ref[g]
        for t in range(S5_CHUNK):
            start = (S5_CHUNK - 1 - t) * hh
            win = x if start == 0 else pltpu.roll(x, width - start, axis=1)
            m_ref[g, t * hh:(t + 1) * hh, :] = win[:, 0:n].astype(BF16)


def _toeplitz_call(lags):
    ngrp, hh, width = lags.shape
    n = S5_CHUNK * hh
    gb = 8
    return pl.pallas_call(
        _toeplitz_kernel,
        grid=(ngrp // gb,),
        in_specs=[pl.BlockSpec((gb, hh, width), lambda i: (i, 0, 0))],
        out_specs=pl.BlockSpec((gb, n, n), lambda i: (i, 0, 0)),
        out_shape=jax.ShapeDtypeStruct((ngrp, n, n), BF16),
        compiler_params=_params(("arbitrary",)),
        name="s5_toeplitz",
    )(lags)


def _s5_chunk_operators(a_re, a_im, log_dt, b_re, b_im, c_re, c_im, reverse):
    hi_p = lax.Precision.HIGHEST
    L = S5_CHUNK
    ngrp, ps, hh = b_re.shape
    dt = jnp.exp(log_dt)[:, None]
    lr, li = a_re * dt, a_im * dt

    def powers(e, xr, xi):
        mag = jnp.exp(e * xr)
        return mag * jnp.cos(e * xi), mag * jnp.sin(e * xi)

    abar_re, abar_im = powers(1.0, lr, li)
    den = a_re * a_re + a_im * a_im
    nr, ni = abar_re - 1.0, abar_im
    f_re = (nr * a_re + ni * a_im) / den
    f_im = (ni * a_re - nr * a_im) / den
    bb_re = f_re[..., None] * b_re - f_im[..., None] * b_im
    bb_im = f_re[..., None] * b_im + f_im[..., None] * b_re

    steps = jnp.arange(L, dtype=F32)
    pk_re, pk_im = powers(steps[None, :, None], lr[:, None, :], li[:, None, :])
    d_re = pk_re[..., None] * bb_re[:, None] - pk_im[..., None] * bb_im[:, None]
    d_im = pk_re[..., None] * bb_im[:, None] + pk_im[..., None] * bb_re[:, None]
    kk = (jnp.einsum("gip,gkpj->gikj", c_re, d_re, precision=hi_p)
          - jnp.einsum("gip,gkpj->gikj", c_im, d_im, precision=hi_p))
    zeros = lambda n: jnp.zeros((ngrp, hh, n, hh), F32)
    lags = (jnp.concatenate([zeros(L - 1), kk, zeros(1)], axis=2) if reverse
            else jnp.concatenate([kk[:, :, ::-1], zeros(L)], axis=2))
    m = _toeplitz_call(lags.reshape(ngrp, hh, 2 * L * hh))

    lr2 = jnp.concatenate([lr, lr], axis=-1)
    li2 = jnp.concatenate([li, li], axis=-1)
    e_p = steps if reverse else (L - 1.0 - steps)
    pr2, pi2 = powers(e_p[None, None, :], lr2[:, :, None], li2[:, :, None])
    pa = jnp.concatenate([bb_re, bb_im], axis=1)
    pb = jnp.concatenate([-bb_im, bb_re], axis=1)
    p = (pr2[..., None] * pa[:, :, None, :] + pi2[..., None] * pb[:, :, None, :]).reshape(ngrp, 2 * ps, L * hh)

    e_q = (L - steps) if reverse else (steps + 1.0)
    qr2, qi2 = powers(e_q[None, :, None], lr2[:, None, :], li2[:, None, :])
    qa = jnp.concatenate([c_re, -c_im], axis=-1)
    qb = jnp.concatenate([-c_im, -c_re], axis=-1)
    q = (qa[:, None] * qr2[:, :, None, :] + qb[:, None] * qi2[:, :, None, :]).reshape(ngrp, L * hh, 2 * ps)

    a16_re, a16_im = powers(float(L), lr, li)
    return (m, p.astype(BF16), q.astype(BF16), a16_re.reshape(ngrp // 2, 1, 2 * ps),
            a16_im.reshape(ngrp // 2, 1, 2 * ps))


def _mix_kernel(x_ref, ga_ref, hf_ref, hb_ref, yf_ref, yb_ref, xs_ref, mod_ref, gpre_ref, gpost_ref,
                wg_ref, bg_ref, wpl_ref, wps_ref, wo_ref, d_ref, wglu_ref, bglu_ref, o_ref, *, colmajor):
    rows, d = x_ref.shape
    tt = rows // SUBLANES
    x3 = x_ref[...].reshape(tt, SUBLANES, d)
    sh1 = mod_ref[:, 0:d][None]
    sc1 = mod_ref[:, d:2 * d][None]
    g1 = mod_ref[:, 2 * d:3 * d][None]
    hn = (_rms(x3) * gpre_ref[...] * (1.0 + sc1) + sh1).reshape(rows, d).astype(BF16)
    zg = _dot(hn, wg_ref[...])

    ha = hf_ref[...].astype(F32) + hb_ref[...].astype(F32)
    ya = jax.nn.gelu(ga_ref[...].astype(F32)) * ha
    pa = _dot(ya.astype(BF16), wpl_ref[...])

    def s5_rows(ref):
        v = ref[:, 0] if colmajor else ref[...]
        return v.reshape(rows, v.shape[-1])

    us = s5_rows(xs_ref)
    hs = s5_rows(yf_ref) + s5_rows(yb_ref)
    vs = jax.nn.gelu(hs + d_ref[...] * us)
    zglu = _dot(vs.astype(BF16), wglu_ref[...])
    tg = jnp.tanh(zg + bg_ref[...])
    ys = vs * jax.nn.sigmoid(zglu + bglu_ref[...])
    pb = _dot(ys.astype(BF16), wps_ref[...])

    mm = (pa + tg[:, 0:d] * pa) + (pb + tg[:, d:2 * d] * pb)
    m = _dot(mm.astype(BF16), wo_ref[...])
    mn = (_rms(m) * gpost_ref[...]).reshape(tt, SUBLANES, d)
    o_ref[...] = (x3 + g1 * mn).reshape(rows, d)


def _mix_call(xtm, ga, hf, hb, yf, yb, xs, mod8, g_pre, g_post, w_gate, b_gate, w_pl, w_ps, w_out,
              s5_d, w_glu, b_glu, colmajor):
    g, n, d = xtm.shape
    tt = T_BLK
    rows = tt * SUBLANES
    nt = n // rows
    ch = xs.shape[-1]
    row_spec = lambda c: pl.BlockSpec((None, rows, c), lambda gi, ti: (gi, ti, 0))
    if colmajor:
        s5_spec = pl.BlockSpec((GRID_W, 1, SUBLANES, ch), lambda gi, ti: (0, ti, 0, 0))
    else:
        s5_spec = pl.BlockSpec((None, tt, SUBLANES, ch), lambda gi, ti: (gi, ti, 0, 0))
    dl = ga.shape[-1]
    consts = [mod8, g_pre.reshape(1, d), g_post.reshape(1, d), w_gate, 0.5 * b_gate.reshape(1, 2 * d), w_pl, w_ps,
              w_out, s5_d.reshape(1, ch), w_glu, b_glu.reshape(1, ch)]
    return pl.pallas_call(
        functools.partial(_mix_kernel, colmajor=colmajor),
        grid=(g, nt),
        in_specs=[row_spec(d), row_spec(dl), row_spec(dl), row_spec(dl), s5_spec, s5_spec, s5_spec]
                 + [_const_spec(a.shape) for a in consts],
        out_specs=row_spec(d),
        out_shape=jax.ShapeDtypeStruct((g, n, d), F32),
        compiler_params=_params(("arbitrary", "arbitrary")),
        name="mix",
    )(xtm, ga, hf, hb, yf, yb, xs, *consts)


FF_CHUNKS = 11


def _ffn_kernel(x_ref, mod_ref, gpre_ref, gpost_ref, w1_ref, w3_ref, w2_ref, o_ref):
    rows, d = x_ref.shape
    tt = rows // SUBLANES
    x3 = x_ref[...].reshape(tt, SUBLANES, d)
    sh2 = mod_ref[:, 3 * d:4 * d][None]
    sc2 = mod_ref[:, 4 * d:5 * d][None]
    g2 = mod_ref[:, 5 * d:6 * d][None]
    hn = (_rms(x3) * gpre_ref[...] * (1.0 + sc2) + sh2).reshape(rows, d).astype(BF16)
    dff = w1_ref.shape[-1]
    cw = dff // FF_CHUNKS
    f = jnp.zeros((rows, d), F32)
    for k in range(FF_CHUNKS):
        cs = slice(k * cw, (k + 1) * cw)
        u1 = _dot(hn, w1_ref[:, cs])
        u3 = _dot(hn, w3_ref[:, cs])
        act = (u1 * jax.nn.sigmoid(u1) * u3).astype(BF16)
        f = f + _dot(act, w2_ref[cs, :])
    fn = (_rms(f) * gpost_ref[...]).reshape(tt, SUBLANES, d)
    o_ref[...] = jnp.swapaxes(x3 + g2 * fn, 0, 1)


def _ffn_call(x1, mod8, g_pre, g_post, w1, w3, w2):
    g, n, d = x1.shape
    tt = T_BLK
    rows = tt * SUBLANES
    nt = n // rows
    consts = [mod8, g_pre.reshape(1, d), g_post.reshape(1, d), w1, w3, w2]
    return pl.pallas_call(
        _ffn_kernel,
        grid=(g, nt),
        in_specs=[pl.BlockSpec((None, rows, d), lambda gi, ti: (gi, ti, 0))]
                 + [_const_spec(a.shape) for a in consts],
        out_specs=pl.BlockSpec((SUBLANES, tt, d), lambda gi, ti: (gi, ti, 0)),
        out_shape=jax.ShapeDtypeStruct((g * SUBLANES, nt * tt, d), F32),
        compiler_params=_params(("arbitrary", "arbitrary")),
        name="ffn",
    )(x1, *consts)


def _block_diag_tiles(w, per_tile):
    nb, k, m = w.shape
    w = w.reshape(nb // per_tile, per_tile, k, m)
    eye = jnp.eye(per_tile, dtype=w.dtype)
    t = jnp.einsum("tbkm,bc->tbkcm", w, eye)
    return t.reshape(nb // per_tile, per_tile * k, per_tile * m)


def _layer(x, mod8, p, h0_lru, h0_re, h0_im, colmajor):
    b, t, d = x.shape
    d_lru = p["conv_b"].shape[-1]
    d_s5 = p["s5_d"].shape[-1]
    xtm, xa, ga, xs = _stage1_call(x, mod8, p["g_pre_mix"], p["w_in"], d_lru, d_s5, colmajor)
    if colmajor:
        xs_seq = xs.reshape(1, GRID_W * xs.shape[1], SUBLANES, d_s5)
    else:
        xs_seq = xs
    hs, fins = [], []
    ys, fr, fi = [], [], []
    hu = None
    for dr in range(2):
        res = _lru_call(xa if dr == 0 else hu, (p["conv_w"], p["conv_b"]) if dr == 0 else None, p["wr_bd"][dr],
                        p["wi_bd"][dr], p["lru_b_r"][dr], p["lru_b_i"][dr], p["lru_lambda"][dr], h0_lru[dr],
                        reverse=bool(dr))
        if dr == 0:
            hu = res[2]
        hs.append(res[0])
        fins.append(res[1])
        y, r_, i_ = _s5_call(xs_seq, *p["s5_ops"][dr], h0_re[dr], h0_im[dr], reverse=bool(dr))
        ys.append(y.reshape(xs.shape))
        fr.append(r_)
        fi.append(i_)
    x1 = _mix_call(xtm, ga, hs[0], hs[1], ys[0], ys[1], xs, mod8, p["g_pre_mix"], p["g_post_mix"], p["w_gate"],
                   p["b_gate"], p["w_proj_lru"], p["w_proj_s5"], p["w_out"], p["s5_d"], p["s5_w_glu"],
                   p["s5_b_glu"], colmajor)
    out = _ffn_call(x1, mod8, p["g_pre_ffn"], p["g_post_ffn"], p["w_ff1"], p["w_ff3"], p["w_ff2"])
    return out, fins, fr, fi


def kernel(x_prompt, x_sample, c, state_lru, state_s5_re, state_s5_im, c_ctx, w_mod, b_mod, g_pre_mix, g_post_mix,
           g_pre_ffn, g_post_ffn, w_in, conv_w, conv_b, lru_w_r, lru_b_r, lru_w_i, lru_b_i, lru_lambda,
           s5_a_re, s5_a_im, s5_log_dt, s5_b_re, s5_b_im, s5_c_re, s5_c_im, s5_d, s5_w_glu, s5_b_glu,
           w_proj_lru, w_proj_s5, w_gate, b_gate, w_out, w_ff_in, w_ff_out):
    depth = w_mod.shape[0]
    n_ctx, _, d = x_prompt.shape
    n_dec = x_sample.shape[0]
    assert n_dec == SUBLANES and n_ctx % SUBLANES == 0
    g_ctx = n_ctx // SUBLANES
    n_dir, n_grp, n_st = s5_a_re.shape[1:]
    d_lru = conv_b.shape[-1]
    d_ff = w_ff_out.shape[1]
    heads_per_tile = MXU_TILE // lru_w_r.shape[-1]

    y_prompt, y_sample = x_prompt, x_sample
    lru_list, re_list, im_list = [], [], []
    for l in range(depth):
        cvecs = jnp.concatenate([c, c_ctx[None], jnp.zeros((SUBLANES - 1, d), F32)], axis=0)
        mod = _mod_call(cvecs, w_mod[l], b_mod[l])
        mod_dec = mod[0:SUBLANES]
        mod_ctx = jnp.broadcast_to(mod[SUBLANES:SUBLANES + 1], (SUBLANES, mod.shape[1]))

        p = {
            "g_pre_mix": g_pre_mix[l], "g_post_mix": g_post_mix[l], "g_pre_ffn": g_pre_ffn[l],
            "g_post_ffn": g_post_ffn[l], "w_in": w_in[l].astype(BF16), "conv_w": conv_w[l], "conv_b": conv_b[l],
            "wr_bd": [_block_diag_tiles(lru_w_r[l, dr], heads_per_tile).astype(BF16) for dr in range(n_dir)],
            "wi_bd": [_block_diag_tiles(lru_w_i[l, dr], heads_per_tile).astype(BF16) for dr in range(n_dir)],
            "lru_b_r": lru_b_r[l], "lru_b_i": lru_b_i[l],
            "lru_lambda": lru_lambda[l],
            "s5_ops": [_s5_chunk_operators(s5_a_re[l, dr], s5_a_im[l, dr], s5_log_dt[l, dr], s5_b_re[l, dr],
                                           s5_b_im[l, dr], s5_c_re[l, dr], s5_c_im[l, dr], reverse=bool(dr))
                       for dr in range(n_dir)],
            "s5_d": s5_d[l], "s5_w_glu": s5_w_glu[l].astype(BF16), "s5_b_glu": s5_b_glu[l],
            "w_proj_lru": w_proj_lru[l].astype(BF16), "w_proj_s5": w_proj_s5[l].astype(BF16),
            "w_gate": (0.5 * w_gate[l]).astype(BF16), "b_gate": b_gate[l], "w_out": (0.5 * w_out[l]).astype(BF16),
            "w_ff1": w_ff_in[l, :, :d_ff].astype(BF16), "w_ff3": w_ff_in[l, :, d_ff:].astype(BF16),
            "w_ff2": w_ff_out[l].astype(BF16),
        }
        ns = n_grp * n_st
        zero_lru = [jnp.zeros((g_ctx, SUBLANES, d_lru), F32)] * n_dir
        zero_s5 = [jnp.zeros((g_ctx, SUBLANES, ns), F32)] * n_dir
        y_prompt, f_lru, f_re, f_im = _layer(y_prompt, mod_ctx, p, zero_lru, zero_s5, zero_s5, colmajor=False)
        lru_list.append(jnp.stack([f.reshape(n_ctx, d_lru) for f in f_lru], axis=1))
        re_list.append(jnp.stack([f.reshape(n_ctx, n_grp, n_st) for f in f_re], axis=1))
        im_list.append(jnp.stack([f.reshape(n_ctx, n_grp, n_st) for f in f_im], axis=1))

        h0_lru = [state_lru[:, l, dr].astype(F32)[None] for dr in range(n_dir)]
        h0_re = [state_s5_re[:, l, dr].reshape(1, n_dec, ns) for dr in range(n_dir)]
        h0_im = [state_s5_im[:, l, dr].reshape(1, n_dec, ns) for dr in range(n_dir)]
        y_sample, _, _, _ = _layer(y_sample, mod_dec, p, h0_lru, h0_re, h0_im, colmajor=True)
    return (y_prompt, y_sample, jnp.stack(lru_list, axis=1), jnp.stack(re_list, axis=1),
            jnp.stack(im_list, axis=1))
```

```python
import functools

import jax
import jax.numpy as jnp
from jax import lax
from jax.experimental import pallas as pl
from jax.experimental.pallas import tpu as pltpu

F32 = jnp.float32
BF16 = jnp.bfloat16

EPS = 1e-6
LRU_C = 8.0
LOG2_E = 1.4426950408889634
GRID_W = 64
SUBLANES = 8
MXU_TILE = 256
S5_GROUP = 16
S5_STATE = 64
VMEM_LIMIT = 56 * 1024 * 1024

T_BLK = 64
LRU_T_BLK = 128
STAGE1_T_BLK = 128


def _const_spec(shape):
    nd = len(shape)
    return pl.BlockSpec(shape, lambda *_: (0,) * nd, pipeline_mode=pl.Buffered(1))


def _params(sem):
    return pltpu.CompilerParams(dimension_semantics=sem, vmem_limit_bytes=VMEM_LIMIT)


def _dot(a, b):
    return jnp.dot(a, b, preferred_element_type=F32)


def _rms(x):
    return x * lax.rsqrt(jnp.mean(x * x, axis=-1, keepdims=True) + EPS)


def _mod_kernel(c_ref, w_ref, b_ref, o_ref):
    c = c_ref[...]
    s = c * jax.nn.sigmoid(c)
    o_ref[...] = jnp.dot(s, w_ref[...], preferred_element_type=F32,
                         precision=lax.Precision.HIGHEST) + b_ref[...]


def _mod_call(cvecs, w_mod, b_mod):
    rows, d = cvecs.shape
    n = w_mod.shape[1]
    nb = 4
    return pl.pallas_call(
        _mod_kernel,
        grid=(nb,),
        in_specs=[pl.BlockSpec((rows, d), lambda j: (0, 0)),
                  pl.BlockSpec((d, n // nb), lambda j: (0, j)),
                  pl.BlockSpec((1, n // nb), lambda j: (0, j))],
        out_specs=pl.BlockSpec((rows, n // nb), lambda j: (0, j)),
        out_shape=jax.ShapeDtypeStruct((rows, n), F32),
        compiler_params=_params(("arbitrary",)),
        name="mod",
    )(cvecs, w_mod, b_mod.reshape(1, n))


def _stage1_kernel(x_ref, mod_ref, g_ref, w_ref, xtm_ref, xa_ref, ga_ref, xs_ref, *, colmajor):
    _, tt, d = x_ref.shape
    rows = tt * SUBLANES
    x = jnp.swapaxes(x_ref[...], 0, 1)
    sh1 = mod_ref[:, 0:d][None]
    sc1 = mod_ref[:, d:2 * d][None]
    hn = _rms(x) * g_ref[...] * (1.0 + sc1) + sh1
    xtm_ref[...] = x.reshape(rows, d)
    z = _dot(hn.reshape(rows, d).astype(BF16), w_ref[...])
    dl = xa_ref.shape[-1]
    xa_ref[...] = z[:, 0:dl].astype(BF16)
    ga_ref[...] = z[:, dl:2 * dl].astype(BF16)
    zs = z[:, 2 * dl:]
    if colmajor:
        per = GRID_W * SUBLANES
        for rp in range(tt // (2 * GRID_W)):
            pair = [zs[(2 * rp + e) * per:(2 * rp + e + 1) * per].reshape(GRID_W, SUBLANES, zs.shape[-1])
                    for e in range(2)]
            xs_ref[:, rp] = jnp.concatenate(pair, axis=1).astype(BF16)
    else:
        xs_ref[...] = zs.astype(BF16)


def _stage1_call(x, mod8, g_pre, w_in, d_lru, d_s5, colmajor):
    b, t, d = x.shape
    g = b // SUBLANES
    tt = STAGE1_T_BLK
    nt = t // tt
    rows = tt * SUBLANES
    row_spec = lambda c: pl.BlockSpec((None, rows, c), lambda gi, ti: (gi, ti, 0))
    if colmajor:
        assert g == 1 and tt % (2 * GRID_W) == 0
        xs_shape = (GRID_W, t // (2 * GRID_W), 2 * SUBLANES, d_s5)
        xs_spec = pl.BlockSpec((GRID_W, tt // (2 * GRID_W), 2 * SUBLANES, d_s5), lambda gi, ti: (0, ti, 0, 0))
    else:
        xs_shape = (g, t * SUBLANES, d_s5)
        xs_spec = row_spec(d_s5)
    return pl.pallas_call(
        functools.partial(_stage1_kernel, colmajor=colmajor),
        grid=(g, nt),
        in_specs=[pl.BlockSpec((SUBLANES, tt, d), lambda gi, ti: (gi, ti, 0)),
                  _const_spec(mod8.shape), _const_spec((1, d)), _const_spec(w_in.shape)],
        out_specs=[row_spec(d), row_spec(d_lru), row_spec(d_lru), xs_spec],
        out_shape=[jax.ShapeDtypeStruct((g, t * SUBLANES, d), F32),
                   jax.ShapeDtypeStruct((g, t * SUBLANES, d_lru), BF16),
                   jax.ShapeDtypeStruct((g, t * SUBLANES, d_lru), BF16),
                   jax.ShapeDtypeStruct(xs_shape, BF16)],
        compiler_params=_params(("arbitrary", "arbitrary")),
        name="stage1",
    )(x, mod8, g_pre.reshape(1, d), w_in)


def _lru_kernel(*refs, reverse, nt, from_conv):
    if from_conv:
        (xm_ref, xp_ref, xn_ref, cw_ref, cb_ref, wr_ref, wi_ref, br_ref, bi_ref, lam_ref, h0_ref,
         h_ref, fin_ref, hu_out_ref, xe_scr, a_scr, b_scr, carry) = refs
        rows, c = xm_ref.shape
    else:
        (hu_ref, wr_ref, wi_ref, br_ref, bi_ref, lam_ref, h0_ref, h_ref, fin_ref, a_scr, b_scr, carry) = refs
        rows, c = hu_ref.shape
    tt = rows // SUBLANES
    i = pl.program_id(1)
    blk = (nt - 1 - i) if reverse else i

    @pl.when(i == 0)
    def _():
        carry[...] = h0_ref[...]

    if from_conv:
        halo = 2 * SUBLANES
        prev_ok = (blk > 0).astype(F32)
        next_ok = (blk < nt - 1).astype(F32)
        xe_scr[0:halo] = xp_ref[...].astype(F32) * prev_ok
        xe_scr[halo:halo + rows] = xm_ref[...].astype(F32)
        xe_scr[halo + rows:halo + rows + SUBLANES] = xn_ref[0:SUBLANES].astype(F32) * next_ok

    for j in range(c // MXU_TILE):
        cs = slice(j * MXU_TILE, (j + 1) * MXU_TILE)
        cc = (-0.5 * LRU_C * LOG2_E) * jax.nn.softplus(-lam_ref[:, cs])
        if from_conv:
            hu = 0.5 * cb_ref[:, cs]
            for k in range(4):
                hu = hu + (0.5 * cw_ref[k:k + 1, cs]) * xe_scr[k * SUBLANES:k * SUBLANES + rows, cs]
            hub = hu.astype(BF16)
            hu_out_ref[:, cs] = hub
        else:
            hub = hu_ref[:, cs]
            hu = hub.astype(F32)
        tr = jnp.tanh(_dot(hub, wr_ref[j]) + 0.5 * br_ref[:, cs])
        ti = jnp.tanh(_dot(hub, wi_ref[j]) + 0.5 * bi_ref[:, cs])
        a = jnp.exp2(cc * tr + cc)
        y = 1.0 - a * a
        a_scr[:, cs] = a
        b_scr[:, cs] = jnp.where(y > 0.0, y * lax.rsqrt(y), 0.0) * (hu * ti + hu)

    def step(s, h):
        t = (tt - 1 - s) if reverse else s
        sl = pl.ds(pl.multiple_of(t * SUBLANES, SUBLANES), SUBLANES)
        h = a_scr[sl, :] * h + b_scr[sl, :]
        b_scr[sl, :] = h
        return h

    h = lax.fori_loop(0, tt, step, carry[...], unroll=8)
    carry[...] = h
    fin_ref[...] = h
    h_ref[...] = b_scr[...].astype(BF16)


def _lru_call(x, conv, wr_bd, wi_bd, b_r, b_i, lam, h0, reverse):
    g, n, c = x.shape
    tt = LRU_T_BLK
    rows = tt * SUBLANES
    nt = n // rows
    blk = (lambda ti: nt - 1 - ti) if reverse else (lambda ti: ti)
    main = pl.BlockSpec((None, rows, c), lambda gi, ti: (gi, blk(ti), 0))
    state = pl.BlockSpec((None, SUBLANES, c), lambda gi, ti: (gi, 0, 0))
    vec = lambda: _const_spec((1, c))
    gate_specs = [_const_spec(wr_bd.shape), _const_spec(wi_bd.shape), vec(), vec(), vec(), state]
    gate_args = (wr_bd, wi_bd, b_r.reshape(1, c), b_i.reshape(1, c), lam.reshape(1, c), h0)
    out_specs = [main, state]
    out_shape = [jax.ShapeDtypeStruct((g, n, c), BF16), jax.ShapeDtypeStruct((g, SUBLANES, c), F32)]
    scratch = [pltpu.VMEM((rows, c), F32), pltpu.VMEM((rows, c), F32), pltpu.VMEM((SUBLANES, c), F32)]
    if conv is not None:
        hb = 2 * SUBLANES
        per = rows // hb
        nhb = n // hb
        prev = pl.BlockSpec((None, hb, c), lambda gi, ti: (gi, jnp.maximum(blk(ti) * per - 1, 0), 0))
        nxt = pl.BlockSpec((None, hb, c), lambda gi, ti: (gi, jnp.minimum((blk(ti) + 1) * per, nhb - 1), 0))
        in_specs = [main, prev, nxt, _const_spec((4, c)), vec()] + gate_specs
        args = (x, x, x, conv[0], conv[1].reshape(1, c)) + gate_args
        out_specs = out_specs + [main]
        out_shape = out_shape + [jax.ShapeDtypeStruct((g, n, c), BF16)]
        scratch = [pltpu.VMEM((rows + 4 * SUBLANES, c), F32)] + scratch
    else:
        in_specs = [main] + gate_specs
        args = (x,) + gate_args
    return pl.pallas_call(
        functools.partial(_lru_kernel, reverse=reverse, nt=nt, from_conv=conv is not None),
        grid=(g, nt),
        in_specs=in_specs,
        out_specs=out_specs,
        out_shape=out_shape,
        scratch_shapes=scratch,
        compiler_params=_params(("arbitrary", "arbitrary")),
        name="lru_bwd" if reverse else "lru_fwd",
    )(*args)


S5_CHUNK = 16
S5_T_BLK = 512
S5_PAIR_UNROLL = 16


def _s5_kernel(xs_ref, m_ref, p_ref, q_ref, ar_ref, ai_ref, h0r_ref, h0i_ref, y_ref, finr_ref, fini_ref,
               xy_scr, st_scr, hin_scr, cr, ci, *, reverse):
    nseq, nchunk = xs_ref.shape[0], xs_ref.shape[1]
    ch = xs_ref.shape[-1]
    n = nseq * nchunk * SUBLANES
    ngrp = ch // S5_GROUP
    npair = ngrp // 2
    kdim = S5_CHUNK * S5_GROUP
    ps = S5_STATE
    unroll = S5_PAIR_UNROLL
    i = pl.program_id(1)

    @pl.when(i == 0)
    def _():
        for j in range(npair):
            for sq in range(nseq):
                cr[j, sq] = h0r_ref[sq, :, 2 * ps * j:2 * ps * (j + 1)]
                ci[j, sq] = h0i_ref[sq, :, 2 * ps * j:2 * ps * (j + 1)]

    for tp in range(S5_CHUNK // 2):
        v = xs_ref[:, :, tp].astype(F32)
        for e in range(2):
            x = v[:, :, e * SUBLANES:(e + 1) * SUBLANES, :].reshape(n, ch).astype(BF16)
            xy_scr[:, 2 * tp + e] = x.T.reshape(ngrp, S5_GROUP, n)

    def pairs(jj, carry_):
        base = jj * unroll
        rs = [_dot(jnp.concatenate([m_ref[2 * base + e], p_ref[2 * base + e]], axis=0),
                   xy_scr[2 * base + e].reshape(kdim, n)) for e in range(2 * unroll)]
        for u in range(unroll):
            r0, r1 = rs[2 * u], rs[2 * u + 1]
            st = jnp.concatenate([r0[kdim:kdim + ps], r1[kdim:kdim + ps], r0[kdim + ps:], r1[kdim + ps:]], axis=0)
            st_scr[u] = st.astype(BF16).T.astype(F32)
        ars = [jnp.broadcast_to(ar_ref[base + u], (SUBLANES, 2 * ps)) for u in range(unroll)]
        ais = [jnp.broadcast_to(ai_ref[base + u], (SUBLANES, 2 * ps)) for u in range(unroll)]
        state = [[(cr[base + u, sq], ci[base + u, sq]) for sq in range(nseq)] for u in range(unroll)]
        for s in range(nchunk):
            m = (nchunk - 1 - s) if reverse else s
            for u in range(unroll):
                for sq in range(nseq):
                    row = (sq * nchunk + m) * SUBLANES
                    rows = slice(row, row + SUBLANES)
                    hr, hi = state[u][sq]
                    hin_scr[u, rows, 0:2 * ps] = hr
                    hin_scr[u, rows, 2 * ps:4 * ps] = hi
                    sr = st_scr[u, rows, 0:2 * ps]
                    si = st_scr[u, rows, 2 * ps:4 * ps]
                    state[u][sq] = (ars[u] * hr - ais[u] * hi + sr, ars[u] * hi + ais[u] * hr + si)
        for u in range(unroll):
            for sq in range(nseq):
                cr[base + u, sq], ci[base + u, sq] = state[u][sq]
        hts = [hin_scr[u].astype(BF16).T for u in range(unroll)]
        for u in range(unroll):
            ht = hts[u]
            for e in range(2):
                het = jnp.concatenate([ht[e * ps:(e + 1) * ps], ht[(2 + e) * ps:(3 + e) * ps]], axis=0)
                y = rs[2 * u + e][0:kdim] + _dot(q_ref[2 * (base + u) + e], het)
                xy_scr[2 * (base + u) + e] = y.astype(BF16).reshape(S5_CHUNK, S5_GROUP, n)
        return carry_

    lax.fori_loop(0, npair // unroll, pairs, 0)

    for j in range(npair):
        for sq in range(nseq):
            finr_ref[sq, :, 2 * ps * j:2 * ps * (j + 1)] = cr[j, sq]
            fini_ref[sq, :, 2 * ps * j:2 * ps * (j + 1)] = ci[j, sq]
    for tp in range(S5_CHUNK // 2):
        pair = [xy_scr[:, 2 * tp + e].reshape(ch, n).T.astype(F32).reshape(nseq, nchunk, SUBLANES, ch)
                for e in range(2)]
        y_ref[:, :, tp] = jnp.concatenate(pair, axis=2).astype(BF16)


def _s5_call(xs, m, p, q, a16_re, a16_im, h0_re, h0_im, reverse):
    g, t8, ch = xs.shape
    t = t8 // SUBLANES
    ngrp = ch // S5_GROUP
    ns = ngrp * S5_STATE
    tt = min(S5_T_BLK, t)
    nt = t // tt
    nseq = min(g, S5_T_BLK // tt)
    nchunk = tt // S5_CHUNK
    n = nseq * nchunk * SUBLANES
    xs5 = xs.reshape(g, t // S5_CHUNK, S5_CHUNK // 2, 2 * SUBLANES, ch)
    blk = (lambda ti: nt - 1 - ti) if reverse else (lambda ti: ti)
    main = pl.BlockSpec((nseq, nchunk, S5_CHUNK // 2, 2 * SUBLANES, ch), lambda gi, ti: (gi, blk(ti), 0, 0, 0))
    st = pl.BlockSpec((nseq, SUBLANES, ns), lambda gi, ti: (gi, 0, 0))
    y, fr, fi = pl.pallas_call(
        functools.partial(_s5_kernel, reverse=reverse),
        grid=(g // nseq, nt),
        in_specs=[main, _const_spec(m.shape), _const_spec(p.shape), _const_spec(q.shape), _const_spec(a16_re.shape),
                  _const_spec(a16_im.shape), st, st],
        out_specs=[main, st, st],
        out_shape=[jax.ShapeDtypeStruct(xs5.shape, BF16), jax.ShapeDtypeStruct((g, SUBLANES, ns), F32),
                   jax.ShapeDtypeStruct((g, SUBLANES, ns), F32)],
        scratch_shapes=[pltpu.VMEM((ngrp, S5_CHUNK, S5_GROUP, n), BF16),
                        pltpu.VMEM((S5_PAIR_UNROLL, n, 4 * S5_STATE), F32),
                        pltpu.VMEM((S5_PAIR_UNROLL, n, 4 * S5_STATE), F32),
                        pltpu.VMEM((ngrp // 2, nseq, SUBLANES, 2 * S5_STATE), F32),
                        pltpu.VMEM((ngrp // 2, nseq, SUBLANES, 2 * S5_STATE), F32)],
        compiler_params=_params(("arbitrary", "arbitrary")),
        name="s5_bwd" if reverse else "s5_fwd",
    )(xs5, m, p, q, a16_re, a16_im, h0_re, h0_im)
    return y.reshape(xs.shape), fr, fi


def _toeplitz_kernel(lag_ref, m_ref, *, reverse):
    ng, hh, width = lag_ref.shape
    n = S5_CHUNK * hh
    q0 = S5_CHUNK if reverse else S5_CHUNK - 1
    for g in range(ng):
        x = lag_ref[g]
        for t in range(S5_CHUNK):
            start = (q0 - t) * hh
            win = x if start == 0 else pltpu.roll(x, width - start, axis=1)
            m_ref[g, t * hh:(t + 1) * hh, :] = win[:, 0:n].astype(BF16)


def _toeplitz_call(lags, reverse):
    ngrp, hh, width = lags.shape
    n = S5_CHUNK * hh
    gb = 8
    return pl.pallas_call(
        functools.partial(_toeplitz_kernel, reverse=reverse),
        grid=(ngrp // gb,),
        in_specs=[pl.BlockSpec((gb, hh, width), lambda i: (i, 0, 0))],
        out_specs=pl.BlockSpec((gb, n, n), lambda i: (i, 0, 0)),
        out_shape=jax.ShapeDtypeStruct((ngrp, n, n), BF16),
        compiler_params=_params(("arbitrary",)),
        name="s5_toeplitz",
    )(lags)


def _s5_chunk_operators(a_re, a_im, log_dt, b_re, b_im, c_re, c_im, reverse):
    hi_p = lax.Precision.HIGHEST
    L = S5_CHUNK
    ngrp, ps, hh = b_re.shape
    dt = jnp.exp(log_dt)[:, None]
    lr, li = a_re * dt, a_im * dt

    def powers(e, xr, xi):
        mag = jnp.exp(e * xr)
        return mag * jnp.cos(e * xi), mag * jnp.sin(e * xi)

    abar_re, abar_im = powers(1.0, lr, li)
    den = a_re * a_re + a_im * a_im
    nr, ni = abar_re - 1.0, abar_im
    f_re = (nr * a_re + ni * a_im) / den
    f_im = (ni * a_re - nr * a_im) / den
    bb_re = f_re[..., None] * b_re - f_im[..., None] * b_im
    bb_im = f_re[..., None] * b_im + f_im[..., None] * b_re

    steps = jnp.arange(L, dtype=F32)
    lag_k = steps if reverse else (L - 1.0 - steps)
    pk_re, pk_im = powers(lag_k[None, :, None], lr[:, None, :], li[:, None, :])
    d_re = pk_re[..., None] * bb_re[:, None] - pk_im[..., None] * bb_im[:, None]
    d_im = pk_re[..., None] * bb_im[:, None] + pk_im[..., None] * bb_re[:, None]
    kk = (jnp.einsum("gip,gkpj->gikj", c_re, d_re, precision=hi_p)
          - jnp.einsum("gip,gkpj->gikj", c_im, d_im, precision=hi_p)).reshape(ngrp, hh, L * hh)
    zeros = jnp.zeros_like(kk)
    lags = jnp.concatenate([zeros, kk] if reverse else [kk, zeros], axis=-1)
    m = _toeplitz_call(lags, reverse)

    lr2 = jnp.concatenate([lr, lr], axis=-1)
    li2 = jnp.concatenate([li, li], axis=-1)
    e_p = steps if reverse else (L - 1.0 - steps)
    pr2, pi2 = powers(e_p[None, None, :], lr2[:, :, None], li2[:, :, None])
    pa = jnp.concatenate([bb_re, bb_im], axis=1)
    pb = jnp.concatenate([-bb_im, bb_re], axis=1)
    p = (pr2[..., None] * pa[:, :, None, :] + pi2[..., None] * pb[:, :, None, :]).reshape(ngrp, 2 * ps, L * hh)

    e_q = (L - steps) if reverse else (steps + 1.0)
    qr2, qi2 = powers(e_q[None, :, None], lr2[:, None, :], li2[:, None, :])
    qa = jnp.concatenate([c_re, -c_im], axis=-1)
    qb = jnp.concatenate([-c_im, -c_re], axis=-1)
    q = (qa[:, None] * qr2[:, :, None, :] + qb[:, None] * qi2[:, :, None, :]).reshape(ngrp, L * hh, 2 * ps)

    a16_re, a16_im = powers(float(L), lr, li)
    return (m, p.astype(BF16), q.astype(BF16), a16_re.reshape(ngrp // 2, 1, 2 * ps),
            a16_im.reshape(ngrp // 2, 1, 2 * ps))


def _mix_kernel(x_ref, ga_ref, hf_ref, hb_ref, yf_ref, yb_ref, xs_ref, mod_ref, gpre_ref, gpost_ref,
                wg_ref, bg_ref, wpl_ref, wps_ref, wo_ref, d_ref, wglu_ref, bglu_ref, o_ref, *, colmajor):
    rows, d = x_ref.shape
    tt = rows // SUBLANES
    x3 = x_ref[...].reshape(tt, SUBLANES, d)
    sh1 = mod_ref[:, 0:d][None]
    sc1 = mod_ref[:, d:2 * d][None]
    g1 = mod_ref[:, 2 * d:3 * d][None]
    hn = (_rms(x3) * gpre_ref[...] * (1.0 + sc1) + sh1).reshape(rows, d).astype(BF16)
    zg = _dot(hn, wg_ref[...])

    ha = hf_ref[...].astype(F32) + hb_ref[...].astype(F32)
    ya = jax.nn.gelu(ga_ref[...].astype(F32)) * ha
    pa = _dot(ya.astype(BF16), wpl_ref[...])

    def s5_rows(ref):
        if not colmajor:
            return ref[...].astype(F32)
        v = ref[:, 0].astype(F32)
        odd = pl.program_id(1) % 2 == 1
        v = jnp.where(odd, v[:, SUBLANES:2 * SUBLANES, :], v[:, 0:SUBLANES, :])
        return v.reshape(rows, v.shape[-1])

    us = s5_rows(xs_ref)
    hs = s5_rows(yf_ref) + s5_rows(yb_ref)
    vs = jax.nn.gelu(hs + d_ref[...] * us)
    zglu = _dot(vs.astype(BF16), wglu_ref[...])
    tg = jnp.tanh(zg + bg_ref[...])
    ys = vs * jax.nn.sigmoid(zglu + bglu_ref[...])
    pb = _dot(ys.astype(BF16), wps_ref[...])

    mm = (pa + tg[:, 0:d] * pa) + (pb + tg[:, d:2 * d] * pb)
    m = _dot(mm.astype(BF16), wo_ref[...])
    mn = (_rms(m) * gpost_ref[...]).reshape(tt, SUBLANES, d)
    o_ref[...] = (x3 + g1 * mn).reshape(rows, d)


def _mix_call(xtm, ga, hf, hb, yf, yb, xs, mod8, g_pre, g_post, w_gate, b_gate, w_pl, w_ps, w_out,
              s5_d, w_glu, b_glu, colmajor):
    g, n, d = xtm.shape
    tt = T_BLK
    rows = tt * SUBLANES
    nt = n // rows
    ch = xs.shape[-1]
    row_spec = lambda c: pl.BlockSpec((None, rows, c), lambda gi, ti: (gi, ti, 0))
    if colmajor:
        assert tt == GRID_W
        s5_spec = pl.BlockSpec((GRID_W, 1, 2 * SUBLANES, ch), lambda gi, ti: (0, ti // 2, 0, 0))
    else:
        s5_spec = row_spec(ch)
    dl = ga.shape[-1]
    consts = [mod8, g_pre.reshape(1, d), g_post.reshape(1, d), w_gate, 0.5 * b_gate.reshape(1, 2 * d), w_pl, w_ps,
              w_out, s5_d.reshape(1, ch), w_glu, b_glu.reshape(1, ch)]
    return pl.pallas_call(
        functools.partial(_mix_kernel, colmajor=colmajor),
        grid=(g, nt),
        in_specs=[row_spec(d), row_spec(dl), row_spec(dl), row_spec(dl), s5_spec, s5_spec, s5_spec]
                 + [_const_spec(a.shape) for a in consts],
        out_specs=row_spec(d),
        out_shape=jax.ShapeDtypeStruct((g, n, d), F32),
        compiler_params=_params(("arbitrary", "arbitrary")),
        name="mix",
    )(xtm, ga, hf, hb, yf, yb, xs, *consts)


FF_CHUNKS = 11


def _ffn_kernel(x_ref, mod_ref, gpre_ref, gpost_ref, w1_ref, w3_ref, w2_ref, o_ref):
    rows, d = x_ref.shape
    tt = rows // SUBLANES
    x3 = x_ref[...].reshape(tt, SUBLANES, d)
    sh2 = mod_ref[:, 3 * d:4 * d][None]
    sc2 = mod_ref[:, 4 * d:5 * d][None]
    g2 = mod_ref[:, 5 * d:6 * d][None]
    hn = (_rms(x3) * gpre_ref[...] * (1.0 + sc2) + sh2).reshape(rows, d).astype(BF16)
    dff = w1_ref.shape[-1]
    cw = dff // FF_CHUNKS
    f = jnp.zeros((rows, d), F32)
    for k in range(FF_CHUNKS):
        cs = slice(k * cw, (k + 1) * cw)
        u1 = _dot(hn, w1_ref[:, cs])
        u3 = _dot(hn, w3_ref[:, cs])
        act = (u1 * jax.nn.sigmoid(u1) * u3).astype(BF16)
        f = f + _dot(act, w2_ref[cs, :])
    fn = (_rms(f) * gpost_ref[...]).reshape(tt, SUBLANES, d)
    o_ref[...] = jnp.swapaxes(x3 + g2 * fn, 0, 1)


def _ffn_call(x1, mod8, g_pre, g_post, w1, w3, w2):
    g, n, d = x1.shape
    tt = T_BLK
    rows = tt * SUBLANES
    nt = n // rows
    consts = [mod8, g_pre.reshape(1, d), g_post.reshape(1, d), w1, w3, w2]
    return pl.pallas_call(
        _ffn_kernel,
        grid=(g, nt),
        in_specs=[pl.BlockSpec((None, rows, d), lambda gi, ti: (gi, ti, 0))]
                 + [_const_spec(a.shape) for a in consts],
        out_specs=pl.BlockSpec((SUBLANES, tt, d), lambda gi, ti: (gi, ti, 0)),
        out_shape=jax.ShapeDtypeStruct((g * SUBLANES, nt * tt, d), F32),
        compiler_params=_params(("arbitrary", "arbitrary")),
        name="ffn",
    )(x1, *consts)


def _block_diag_tiles(w, per_tile):
    nb, k, m = w.shape
    w = w.reshape(nb // per_tile, per_tile, k, m)
    eye = jnp.eye(per_tile, dtype=w.dtype)
    t = jnp.einsum("tbkm,bc->tbkcm", w, eye)
    return t.reshape(nb // per_tile, per_tile * k, per_tile * m)


def _layer(x, mod8, p, h0_lru, h0_re, h0_im, colmajor):
    b, t, d = x.shape
    d_lru = p["conv_b"].shape[-1]
    d_s5 = p["s5_d"].shape[-1]
    xtm, xa, ga, xs = _stage1_call(x, mod8, p["g_pre_mix"], p["w_in"], d_lru, d_s5, colmajor)
    xs_seq = xs.reshape(1, -1, d_s5) if colmajor else xs
    hs, fins = [], []
    ys, fr, fi = [], [], []
    hu = None
    for dr in range(2):
        res = _lru_call(xa if dr == 0 else hu, (p["conv_w"], p["conv_b"]) if dr == 0 else None, p["wr_bd"][dr],
                        p["wi_bd"][dr], p["lru_b_r"][dr], p["lru_b_i"][dr], p["lru_lambda"][dr], h0_lru[dr],
                        reverse=bool(dr))
        if dr == 0:
            hu = res[2]
        hs.append(res[0])
        fins.append(res[1])
        y, r_, i_ = _s5_call(xs_seq, *p["s5_ops"][dr], h0_re[dr], h0_im[dr], reverse=bool(dr))
        ys.append(y.reshape(xs.shape))
        fr.append(r_)
        fi.append(i_)
    x1 = _mix_call(xtm, ga, hs[0], hs[1], ys[0], ys[1], xs, mod8, p["g_pre_mix"], p["g_post_mix"], p["w_gate"],
                   p["b_gate"], p["w_proj_lru"], p["w_proj_s5"], p["w_out"], p["s5_d"], p["s5_w_glu"],
                   p["s5_b_glu"], colmajor)
    out = _ffn_call(x1, mod8, p["g_pre_ffn"], p["g_post_ffn"], p["w_ff1"], p["w_ff3"], p["w_ff2"])
    return out, fins, fr, fi


def kernel(x_prompt, x_sample, c, state_lru, state_s5_re, state_s5_im, c_ctx, w_mod, b_mod, g_pre_mix, g_post_mix,
           g_pre_ffn, g_post_ffn, w_in, conv_w, conv_b, lru_w_r, lru_b_r, lru_w_i, lru_b_i, lru_lambda,
           s5_a_re, s5_a_im, s5_log_dt, s5_b_re, s5_b_im, s5_c_re, s5_c_im, s5_d, s5_w_glu, s5_b_glu,
           w_proj_lru, w_proj_s5, w_gate, b_gate, w_out, w_ff_in, w_ff_out):
    depth = w_mod.shape[0]
    n_ctx, _, d = x_prompt.shape
    n_dec = x_sample.shape[0]
    assert n_dec == SUBLANES and n_ctx % SUBLANES == 0
    g_ctx = n_ctx // SUBLANES
    n_dir, n_grp, n_st = s5_a_re.shape[1:]
    d_lru = conv_b.shape[-1]
    d_ff = w_ff_out.shape[1]
    heads_per_tile = MXU_TILE // lru_w_r.shape[-1]

    y_prompt, y_sample = x_prompt, x_sample
    lru_list, re_list, im_list = [], [], []
    for l in range(depth):
        cvecs = jnp.concatenate([c, c_ctx[None], jnp.zeros((SUBLANES - 1, d), F32)], axis=0)
        mod = _mod_call(cvecs, w_mod[l], b_mod[l])
        mod_dec = mod[0:SUBLANES]
        mod_ctx = jnp.broadcast_to(mod[SUBLANES:SUBLANES + 1], (SUBLANES, mod.shape[1]))

        p = {
            "g_pre_mix": g_pre_mix[l], "g_post_mix": g_post_mix[l], "g_pre_ffn": g_pre_ffn[l],
            "g_post_ffn": g_post_ffn[l], "w_in": w_in[l].astype(BF16), "conv_w": conv_w[l], "conv_b": conv_b[l],
            "wr_bd": [_block_diag_tiles(lru_w_r[l, dr], heads_per_tile).astype(BF16) for dr in range(n_dir)],
            "wi_bd": [_block_diag_tiles(lru_w_i[l, dr], heads_per_tile).astype(BF16) for dr in range(n_dir)],
            "lru_b_r": lru_b_r[l], "lru_b_i": lru_b_i[l],
            "lru_lambda": lru_lambda[l],
            "s5_ops": [_s5_chunk_operators(s5_a_re[l, dr], s5_a_im[l, dr], s5_log_dt[l, dr], s5_b_re[l, dr],
                                           s5_b_im[l, dr], s5_c_re[l, dr], s5_c_im[l, dr], reverse=bool(dr))
                       for dr in range(n_dir)],
            "s5_d": s5_d[l], "s5_w_glu": s5_w_glu[l].astype(BF16), "s5_b_glu": s5_b_glu[l],
            "w_proj_lru": w_proj_lru[l].astype(BF16), "w_proj_s5": w_proj_s5[l].astype(BF16),
            "w_gate": (0.5 * w_gate[l]).astype(BF16), "b_gate": b_gate[l], "w_out": (0.5 * w_out[l]).astype(BF16),
            "w_ff1": w_ff_in[l, :, :d_ff].astype(BF16), "w_ff3": w_ff_in[l, :, d_ff:].astype(BF16),
            "w_ff2": w_ff_out[l].astype(BF16),
        }
        ns = n_grp * n_st
        zero_lru = [jnp.zeros((g_ctx, SUBLANES, d_lru), F32)] * n_dir
        zero_s5 = [jnp.zeros((g_ctx, SUBLANES, ns), F32)] * n_dir
        y_prompt, f_lru, f_re, f_im = _layer(y_prompt, mod_ctx, p, zero_lru, zero_s5, zero_s5, colmajor=False)
        lru_list.append(jnp.stack([f.reshape(n_ctx, d_lru) for f in f_lru], axis=1))
        re_list.append(jnp.stack([f.reshape(n_ctx, n_grp, n_st) for f in f_re], axis=1))
        im_list.append(jnp.stack([f.reshape(n_ctx, n_grp, n_st) for f in f_im], axis=1))

        h0_lru = [state_lru[:, l, dr].astype(F32)[None] for dr in range(n_dir)]
        h0_re = [state_s5_re[:, l, dr].reshape(1, n_dec, ns) for dr in range(n_dir)]
        h0_im = [state_s5_im[:, l, dr].reshape(1, n_dec, ns) for dr in range(n_dir)]
        y_sample, _, _, _ = _layer(y_sample, mod_dec, p, h0_lru, h0_re, h0_im, colmajor=True)
    return (y_prompt, y_sample, jnp.stack(lru_list, axis=1), jnp.stack(re_list, axis=1),
            jnp.stack(im_list, axis=1))
```

```python
import functools

import jax
import jax.numpy as jnp
from jax import lax
from jax.experimental import pallas as pl
from jax.experimental.pallas import tpu as pltpu

F32 = jnp.float32
BF16 = jnp.bfloat16

EPS = 1e-6
LRU_C = 8.0
LOG2_E = 1.4426950408889634
GRID_W = 64
SUBLANES = 8
MXU_TILE = 256
S5_GROUP = 16
S5_STATE = 64
VMEM_LIMIT = 56 * 1024 * 1024

T_BLK = 64
LRU_T_BLK = 128
STAGE1_T_BLK = 128


def _const_spec(shape):
    nd = len(shape)
    return pl.BlockSpec(shape, lambda *_: (0,) * nd, pipeline_mode=pl.Buffered(1))


def _params(sem):
    return pltpu.CompilerParams(dimension_semantics=sem, vmem_limit_bytes=VMEM_LIMIT)


def _dot(a, b):
    return jnp.dot(a, b, preferred_element_type=F32)


def _rms(x):
    return x * lax.rsqrt(jnp.mean(x * x, axis=-1, keepdims=True) + EPS)


def _mod_kernel(c_ref, w_ref, b_ref, o_ref):
    c = c_ref[...]
    s = c * jax.nn.sigmoid(c)
    o_ref[...] = jnp.dot(s, w_ref[...], preferred_element_type=F32,
                         precision=lax.Precision.HIGHEST) + b_ref[...]


def _mod_call(cvecs, w_mod, b_mod):
    rows, d = cvecs.shape
    n = w_mod.shape[1]
    nb = 4
    return pl.pallas_call(
        _mod_kernel,
        grid=(nb,),
        in_specs=[pl.BlockSpec((rows, d), lambda j: (0, 0)),
                  pl.BlockSpec((d, n // nb), lambda j: (0, j)),
                  pl.BlockSpec((1, n // nb), lambda j: (0, j))],
        out_specs=pl.BlockSpec((rows, n // nb), lambda j: (0, j)),
        out_shape=jax.ShapeDtypeStruct((rows, n), F32),
        compiler_params=_params(("arbitrary",)),
        name="mod",
    )(cvecs, w_mod, b_mod.reshape(1, n))


def _stage1_kernel(x_ref, mod_ref, g_ref, w_ref, xtm_ref, xa_ref, ga_ref, xs_ref):
    _, tt, d = x_ref.shape
    rows = tt * SUBLANES
    x = jnp.swapaxes(x_ref[...], 0, 1)
    sh1 = mod_ref[:, 0:d][None]
    sc1 = mod_ref[:, d:2 * d][None]
    hn = _rms(x) * g_ref[...] * (1.0 + sc1) + sh1
    xtm_ref[...] = x.reshape(rows, d)
    z = _dot(hn.reshape(rows, d).astype(BF16), w_ref[...])
    dl = xa_ref.shape[-1]
    xa_ref[...] = z[:, 0:dl].astype(BF16)
    ga_ref[...] = z[:, dl:2 * dl].astype(BF16)
    zs = z[:, 2 * dl:]
    xs_ref[...] = zs.astype(BF16).reshape(xs_ref.shape)


def _stage1_call(x, mod8, g_pre, w_in, d_lru, d_s5, colmajor):
    b, t, d = x.shape
    g = b // SUBLANES
    tt = STAGE1_T_BLK
    nt = t // tt
    rows = tt * SUBLANES
    row_spec = lambda c: pl.BlockSpec((None, rows, c), lambda gi, ti: (gi, ti, 0))
    if colmajor:
        assert g == 1 and tt % GRID_W == 0
        per = GRID_W * SUBLANES
        xs_shape = (t // GRID_W, per, d_s5)
        xs_spec = pl.BlockSpec((tt // GRID_W, per, d_s5), lambda gi, ti: (ti, 0, 0))
    else:
        xs_shape = (g, t * SUBLANES, d_s5)
        xs_spec = row_spec(d_s5)
    return pl.pallas_call(
        _stage1_kernel,
        grid=(g, nt),
        in_specs=[pl.BlockSpec((SUBLANES, tt, d), lambda gi, ti: (gi, ti, 0)),
                  _const_spec(mod8.shape), _const_spec((1, d)), _const_spec(w_in.shape)],
        out_specs=[row_spec(d), row_spec(d_lru), row_spec(d_lru), xs_spec],
        out_shape=[jax.ShapeDtypeStruct((g, t * SUBLANES, d), F32),
                   jax.ShapeDtypeStruct((g, t * SUBLANES, d_lru), BF16),
                   jax.ShapeDtypeStruct((g, t * SUBLANES, d_lru), BF16),
                   jax.ShapeDtypeStruct(xs_shape, BF16)],
        compiler_params=_params(("arbitrary", "arbitrary")),
        name="stage1",
    )(x, mod8, g_pre.reshape(1, d), w_in)


def _lru_kernel(*refs, reverse, nt, from_conv):
    if from_conv:
        (xm_ref, xp_ref, xn_ref, cw_ref, cb_ref, wr_ref, wi_ref, br_ref, bi_ref, lam_ref, h0_ref,
         h_ref, fin_ref, hu_out_ref, xe_scr, a_scr, b_scr, carry) = refs
        rows, c = xm_ref.shape
    else:
        (hu_ref, wr_ref, wi_ref, br_ref, bi_ref, lam_ref, h0_ref, h_ref, fin_ref, a_scr, b_scr, carry) = refs
        rows, c = hu_ref.shape
    tt = rows // SUBLANES
    i = pl.program_id(1)
    blk = (nt - 1 - i) if reverse else i

    @pl.when(i == 0)
    def _():
        carry[...] = h0_ref[...]

    if from_conv:
        halo = 2 * SUBLANES
        prev_ok = (blk > 0).astype(F32)
        next_ok = (blk < nt - 1).astype(F32)
        xe_scr[0:halo] = xp_ref[...].astype(F32) * prev_ok
        xe_scr[halo:halo + rows] = xm_ref[...].astype(F32)
        xe_scr[halo + rows:halo + rows + SUBLANES] = xn_ref[0:SUBLANES].astype(F32) * next_ok

    for j in range(c // MXU_TILE):
        cs = slice(j * MXU_TILE, (j + 1) * MXU_TILE)
        cc = (-0.5 * LRU_C * LOG2_E) * jax.nn.softplus(-lam_ref[:, cs])
        if from_conv:
            hu = 0.5 * cb_ref[:, cs]
            for k in range(4):
                hu = hu + (0.5 * cw_ref[k:k + 1, cs]) * xe_scr[k * SUBLANES:k * SUBLANES + rows, cs]
            hub = hu.astype(BF16)
            hu_out_ref[:, cs] = hub
        else:
            hub = hu_ref[:, cs]
            hu = hub.astype(F32)
        tr = jnp.tanh(_dot(hub, wr_ref[j]) + 0.5 * br_ref[:, cs])
        ti = jnp.tanh(_dot(hub, wi_ref[j]) + 0.5 * bi_ref[:, cs])
        a = jnp.exp2(cc * tr + cc)
        y = 1.0 - a * a
        a_scr[:, cs] = a
        b_scr[:, cs] = jnp.where(y > 0.0, y * lax.rsqrt(y), 0.0) * (hu * ti + hu)

    def step(s, h):
        t = (tt - 1 - s) if reverse else s
        sl = pl.ds(pl.multiple_of(t * SUBLANES, SUBLANES), SUBLANES)
        h = a_scr[sl, :] * h + b_scr[sl, :]
        b_scr[sl, :] = h
        return h

    h = lax.fori_loop(0, tt, step, carry[...], unroll=8)
    carry[...] = h
    fin_ref[...] = h
    h_ref[...] = b_scr[...].astype(BF16)


def _lru_call(x, conv, wr_bd, wi_bd, b_r, b_i, lam, h0, reverse):
    g, n, c = x.shape
    tt = LRU_T_BLK
    rows = tt * SUBLANES
    nt = n // rows
    blk = (lambda ti: nt - 1 - ti) if reverse else (lambda ti: ti)
    main = pl.BlockSpec((None, rows, c), lambda gi, ti: (gi, blk(ti), 0))
    state = pl.BlockSpec((None, SUBLANES, c), lambda gi, ti: (gi, 0, 0))
    vec = lambda: _const_spec((1, c))
    gate_specs = [_const_spec(wr_bd.shape), _const_spec(wi_bd.shape), vec(), vec(), vec(), state]
    gate_args = (wr_bd, wi_bd, b_r.reshape(1, c), b_i.reshape(1, c), lam.reshape(1, c), h0)
    out_specs = [main, state]
    out_shape = [jax.ShapeDtypeStruct((g, n, c), BF16), jax.ShapeDtypeStruct((g, SUBLANES, c), F32)]
    scratch = [pltpu.VMEM((rows, c), F32), pltpu.VMEM((rows, c), F32), pltpu.VMEM((SUBLANES, c), F32)]
    if conv is not None:
        hb = 2 * SUBLANES
        per = rows // hb
        nhb = n // hb
        prev = pl.BlockSpec((None, hb, c), lambda gi, ti: (gi, jnp.maximum(blk(ti) * per - 1, 0), 0))
        nxt = pl.BlockSpec((None, hb, c), lambda gi, ti: (gi, jnp.minimum((blk(ti) + 1) * per, nhb - 1), 0))
        in_specs = [main, prev, nxt, _const_spec((4, c)), vec()] + gate_specs
        args = (x, x, x, conv[0], conv[1].reshape(1, c)) + gate_args
        out_specs = out_specs + [main]
        out_shape = out_shape + [jax.ShapeDtypeStruct((g, n, c), BF16)]
        scratch = [pltpu.VMEM((rows + 4 * SUBLANES, c), F32)] + scratch
    else:
        in_specs = [main] + gate_specs
        args = (x,) + gate_args
    return pl.pallas_call(
        functools.partial(_lru_kernel, reverse=reverse, nt=nt, from_conv=conv is not None),
        grid=(g, nt),
        in_specs=in_specs,
        out_specs=out_specs,
        out_shape=out_shape,
        scratch_shapes=scratch,
        compiler_params=_params(("arbitrary", "arbitrary")),
        name="lru_bwd" if reverse else "lru_fwd",
    )(*args)


S5_CHUNK = 16
S5_T_BLK = 512
S5_PAIR_UNROLL = 16


def _s5_kernel(xs_ref, m_ref, p_ref, q_ref, ar_ref, ai_ref, h0r_ref, h0i_ref, y_ref, finr_ref, fini_ref,
               xy_scr, st_scr, hin_scr, cr, ci, *, reverse, colmajor):
    ch = xs_ref.shape[-1]
    if colmajor:
        nseq = 1
        nrow, ncol = xs_ref.shape[0], xs_ref.shape[1] // SUBLANES
        nrh = nrow // S5_CHUNK
        nchunk = ncol * nrh
        chunk_row = lambda sq, m: ((m % nrh) * ncol + m // nrh) * SUBLANES
    else:
        nseq, nchunk = xs_ref.shape[0], xs_ref.shape[1]
        chunk_row = lambda sq, m: (sq * nchunk + m) * SUBLANES
    n = nseq * nchunk * SUBLANES
    ngrp = ch // S5_GROUP
    npair = ngrp // 2
    kdim = S5_CHUNK * S5_GROUP
    ps = S5_STATE
    unroll = S5_PAIR_UNROLL
    i = pl.program_id(1)

    @pl.when(i == 0)
    def _():
        for j in range(npair):
            for sq in range(nseq):
                cr[j, sq] = h0r_ref[sq, :, 2 * ps * j:2 * ps * (j + 1)]
                ci[j, sq] = h0i_ref[sq, :, 2 * ps * j:2 * ps * (j + 1)]

    if colmajor:
        for tau in range(S5_CHUNK):
            x = jnp.concatenate([xs_ref[rh * S5_CHUNK + tau] for rh in range(nrh)], axis=0)
            xy_scr[:, tau] = x.T.reshape(ngrp, S5_GROUP, n)
    else:
        for tp in range(S5_CHUNK // 2):
            v = xs_ref[:, :, tp].astype(F32)
            for e in range(2):
                x = v[:, :, e * SUBLANES:(e + 1) * SUBLANES, :].reshape(n, ch).astype(BF16)
                xy_scr[:, 2 * tp + e] = x.T.reshape(ngrp, S5_GROUP, n)

    def pairs(jj, carry_):
        base = jj * unroll
        rs = [_dot(jnp.concatenate([m_ref[2 * base + e], p_ref[2 * base + e]], axis=0),
                   xy_scr[2 * base + e].reshape(kdim, n)) for e in range(2 * unroll)]
        for u in range(unroll):
            r0, r1 = rs[2 * u], rs[2 * u + 1]
            st = jnp.concatenate([r0[kdim:kdim + ps], r1[kdim:kdim + ps], r0[kdim + ps:], r1[kdim + ps:]], axis=0)
            st_scr[u] = st.astype(BF16).T.astype(F32)
        ars = [jnp.broadcast_to(ar_ref[base + u], (SUBLANES, 2 * ps)) for u in range(unroll)]
        ais = [jnp.broadcast_to(ai_ref[base + u], (SUBLANES, 2 * ps)) for u in range(unroll)]
        state = [[(cr[base + u, sq], ci[base + u, sq]) for sq in range(nseq)] for u in range(unroll)]
        for s in range(nchunk):
            m = (nchunk - 1 - s) if reverse else s
            for u in range(unroll):
                for sq in range(nseq):
                    row = chunk_row(sq, m)
                    rows = slice(row, row + SUBLANES)
                    hr, hi = state[u][sq]
                    hin_scr[u, rows, 0:2 * ps] = hr
                    hin_scr[u, rows, 2 * ps:4 * ps] = hi
                    sr = st_scr[u, rows, 0:2 * ps]
                    si = st_scr[u, rows, 2 * ps:4 * ps]
                    state[u][sq] = (ars[u] * hr - ais[u] * hi + sr, ars[u] * hi + ais[u] * hr + si)
        for u in range(unroll):
            for sq in range(nseq):
                cr[base + u, sq], ci[base + u, sq] = state[u][sq]
        hts = [hin_scr[u].astype(BF16).T for u in range(unroll)]
        for u in range(unroll):
            ht = hts[u]
            for e in range(2):
                het = jnp.concatenate([ht[e * ps:(e + 1) * ps], ht[(2 + e) * ps:(3 + e) * ps]], axis=0)
                y = rs[2 * u + e][0:kdim] + _dot(q_ref[2 * (base + u) + e], het)
                xy_scr[2 * (base + u) + e] = y.astype(BF16).reshape(S5_CHUNK, S5_GROUP, n)
        return carry_

    lax.fori_loop(0, npair // unroll, pairs, 0)

    for j in range(npair):
        for sq in range(nseq):
            finr_ref[sq, :, 2 * ps * j:2 * ps * (j + 1)] = cr[j, sq]
            fini_ref[sq, :, 2 * ps * j:2 * ps * (j + 1)] = ci[j, sq]
    if colmajor:
        per = ncol * SUBLANES
        for tau in range(S5_CHUNK):
            yt = xy_scr[:, tau].reshape(ch, n).T
            for rh in range(nrh):
                y_ref[rh * S5_CHUNK + tau] = yt[rh * per:(rh + 1) * per]
    else:
        for tp in range(S5_CHUNK // 2):
            pair = [xy_scr[:, 2 * tp + e].reshape(ch, n).T.astype(F32).reshape(nseq, nchunk, SUBLANES, ch)
                    for e in range(2)]
            y_ref[:, :, tp] = jnp.concatenate(pair, axis=2).astype(BF16)


def _s5_call(xs, m, p, q, a16_re, a16_im, h0_re, h0_im, reverse, colmajor):
    ch = xs.shape[-1]
    ngrp = ch // S5_GROUP
    ns = ngrp * S5_STATE
    if colmajor:
        nrow = xs.shape[0]
        g, nseq = 1, 1
        ncol = S5_T_BLK // nrow
        nt = xs.shape[1] // (ncol * SUBLANES)
        n = S5_T_BLK // S5_CHUNK * SUBLANES
        blk = (lambda ti: nt - 1 - ti) if reverse else (lambda ti: ti)
        arr = xs
        main = pl.BlockSpec((nrow, ncol * SUBLANES, ch), lambda gi, ti: (0, blk(ti), 0))
    else:
        g, t8, _ = xs.shape
        t = t8 // SUBLANES
        tt = min(S5_T_BLK, t)
        nt = t // tt
        nseq = min(g, S5_T_BLK // tt)
        nchunk = tt // S5_CHUNK
        n = nseq * nchunk * SUBLANES
        blk = (lambda ti: nt - 1 - ti) if reverse else (lambda ti: ti)
        arr = xs.reshape(g, t // S5_CHUNK, S5_CHUNK // 2, 2 * SUBLANES, ch)
        main = pl.BlockSpec((nseq, nchunk, S5_CHUNK // 2, 2 * SUBLANES, ch), lambda gi, ti: (gi, blk(ti), 0, 0, 0))
    st = pl.BlockSpec((nseq, SUBLANES, ns), lambda gi, ti: (gi, 0, 0))
    y, fr, fi = pl.pallas_call(
        functools.partial(_s5_kernel, reverse=reverse, colmajor=colmajor),
        grid=(g // nseq, nt),
        in_specs=[main, _const_spec(m.shape), _const_spec(p.shape), _const_spec(q.shape), _const_spec(a16_re.shape),
                  _const_spec(a16_im.shape), st, st],
        out_specs=[main, st, st],
        out_shape=[jax.ShapeDtypeStruct(arr.shape, BF16), jax.ShapeDtypeStruct((g, SUBLANES, ns), F32),
                   jax.ShapeDtypeStruct((g, SUBLANES, ns), F32)],
        scratch_shapes=[pltpu.VMEM((ngrp, S5_CHUNK, S5_GROUP, n), BF16),
                        pltpu.VMEM((S5_PAIR_UNROLL, n, 4 * S5_STATE), F32),
                        pltpu.VMEM((S5_PAIR_UNROLL, n, 4 * S5_STATE), F32),
                        pltpu.VMEM((ngrp // 2, nseq, SUBLANES, 2 * S5_STATE), F32),
                        pltpu.VMEM((ngrp // 2, nseq, SUBLANES, 2 * S5_STATE), F32)],
        compiler_params=_params(("arbitrary", "arbitrary")),
        name="s5_bwd" if reverse else "s5_fwd",
    )(arr, m, p, q, a16_re, a16_im, h0_re, h0_im)
    return y.reshape(xs.shape), fr, fi


def _toeplitz_kernel(lag_ref, m_ref, *, reverse):
    ng, hh, width = lag_ref.shape
    n = S5_CHUNK * hh
    q0 = S5_CHUNK if reverse else S5_CHUNK - 1
    for g in range(ng):
        x = lag_ref[g]
        for t in range(S5_CHUNK):
            start = (q0 - t) * hh
            win = x if start == 0 else pltpu.roll(x, width - start, axis=1)
            m_ref[g, t * hh:(t + 1) * hh, :] = win[:, 0:n].astype(BF16)


def _toeplitz_call(lags, reverse):
    ngrp, hh, width = lags.shape
    n = S5_CHUNK * hh
    gb = 8
    return pl.pallas_call(
        functools.partial(_toeplitz_kernel, reverse=reverse),
        grid=(ngrp // gb,),
        in_specs=[pl.BlockSpec((gb, hh, width), lambda i: (i, 0, 0))],
        out_specs=pl.BlockSpec((gb, n, n), lambda i: (i, 0, 0)),
        out_shape=jax.ShapeDtypeStruct((ngrp, n, n), BF16),
        compiler_params=_params(("arbitrary",)),
        name="s5_toeplitz",
    )(lags)


def _s5_chunk_operators(a_re, a_im, log_dt, b_re, b_im, c_re, c_im, reverse):
    hi_p = lax.Precision.HIGHEST
    L = S5_CHUNK
    ngrp, ps, hh = b_re.shape
    dt = jnp.exp(log_dt)[:, None]
    lr, li = a_re * dt, a_im * dt

    def powers(e, xr, xi):
        mag = jnp.exp(e * xr)
        return mag * jnp.cos(e * xi), mag * jnp.sin(e * xi)

    abar_re, abar_im = powers(1.0, lr, li)
    den = a_re * a_re + a_im * a_im
    nr, ni = abar_re - 1.0, abar_im
    f_re = (nr * a_re + ni * a_im) / den
    f_im = (ni * a_re - nr * a_im) / den
    bb_re = f_re[..., None] * b_re - f_im[..., None] * b_im
    bb_im = f_re[..., None] * b_im + f_im[..., None] * b_re

    steps = jnp.arange(L, dtype=F32)
    lag_k = steps if reverse else (L - 1.0 - steps)
    pk_re, pk_im = powers(lag_k[None, :, None], lr[:, None, :], li[:, None, :])
    d_re = pk_re[..., None] * bb_re[:, None] - pk_im[..., None] * bb_im[:, None]
    d_im = pk_re[..., None] * bb_im[:, None] + pk_im[..., None] * bb_re[:, None]
    kk = (jnp.einsum("gip,gkpj->gikj", c_re, d_re, precision=hi_p)
          - jnp.einsum("gip,gkpj->gikj", c_im, d_im, precision=hi_p)).reshape(ngrp, hh, L * hh)
    zeros = jnp.zeros_like(kk)
    lags = jnp.concatenate([zeros, kk] if reverse else [kk, zeros], axis=-1)
    m = _toeplitz_call(lags, reverse)

    lr2 = jnp.concatenate([lr, lr], axis=-1)
    li2 = jnp.concatenate([li, li], axis=-1)
    e_p = steps if reverse else (L - 1.0 - steps)
    pr2, pi2 = powers(e_p[None, None, :], lr2[:, :, None], li2[:, :, None])
    pa = jnp.concatenate([bb_re, bb_im], axis=1)
    pb = jnp.concatenate([-bb_im, bb_re], axis=1)
    p = (pr2[..., None] * pa[:, :, None, :] + pi2[..., None] * pb[:, :, None, :]).reshape(ngrp, 2 * ps, L * hh)

    e_q = (L - steps) if reverse else (steps + 1.0)
    qr2, qi2 = powers(e_q[None, :, None], lr2[:, None, :], li2[:, None, :])
    qa = jnp.concatenate([c_re, -c_im], axis=-1)
    qb = jnp.concatenate([-c_im, -c_re], axis=-1)
    q = (qa[:, None] * qr2[:, :, None, :] + qb[:, None] * qi2[:, :, None, :]).reshape(ngrp, L * hh, 2 * ps)

    a16_re, a16_im = powers(float(L), lr, li)
    return (m, p.astype(BF16), q.astype(BF16), a16_re.reshape(ngrp // 2, 1, 2 * ps),
            a16_im.reshape(ngrp // 2, 1, 2 * ps))


def _mix_kernel(x_ref, ga_ref, hf_ref, hb_ref, yf_ref, yb_ref, xs_ref, mod_ref, gpre_ref, gpost_ref,
                wg_ref, bg_ref, wpl_ref, wps_ref, wo_ref, d_ref, wglu_ref, bglu_ref, o_ref):
    rows, d = x_ref.shape
    tt = rows // SUBLANES
    x3 = x_ref[...].reshape(tt, SUBLANES, d)
    sh1 = mod_ref[:, 0:d][None]
    sc1 = mod_ref[:, d:2 * d][None]
    g1 = mod_ref[:, 2 * d:3 * d][None]
    hn = (_rms(x3) * gpre_ref[...] * (1.0 + sc1) + sh1).reshape(rows, d).astype(BF16)
    zg = _dot(hn, wg_ref[...])

    ha = hf_ref[...].astype(F32) + hb_ref[...].astype(F32)
    ya = jax.nn.gelu(ga_ref[...].astype(F32)) * ha
    pa = _dot(ya.astype(BF16), wpl_ref[...])

    us = xs_ref[...].astype(F32)
    hs = yf_ref[...].astype(F32) + yb_ref[...].astype(F32)
    vs = jax.nn.gelu(hs + d_ref[...] * us)
    zglu = _dot(vs.astype(BF16), wglu_ref[...])
    tg = jnp.tanh(zg + bg_ref[...])
    ys = vs * jax.nn.sigmoid(zglu + bglu_ref[...])
    pb = _dot(ys.astype(BF16), wps_ref[...])

    mm = (pa + tg[:, 0:d] * pa) + (pb + tg[:, d:2 * d] * pb)
    m = _dot(mm.astype(BF16), wo_ref[...])
    mn = (_rms(m) * gpost_ref[...]).reshape(tt, SUBLANES, d)
    o_ref[...] = (x3 + g1 * mn).reshape(rows, d)


def _mix_call(xtm, ga, hf, hb, yf, yb, xs, mod8, g_pre, g_post, w_gate, b_gate, w_pl, w_ps, w_out,
              s5_d, w_glu, b_glu, colmajor):
    g, n, d = xtm.shape
    tt = T_BLK
    rows = tt * SUBLANES
    nt = n // rows
    ch = xs.shape[-1]
    row_spec = lambda c: pl.BlockSpec((None, rows, c), lambda gi, ti: (gi, ti, 0))
    if colmajor:
        assert tt == GRID_W
        s5_spec = pl.BlockSpec((None, rows, ch), lambda gi, ti: (ti, 0, 0))
    else:
        s5_spec = row_spec(ch)
    dl = ga.shape[-1]
    consts = [mod8, g_pre.reshape(1, d), g_post.reshape(1, d), w_gate, 0.5 * b_gate.reshape(1, 2 * d), w_pl, w_ps,
              w_out, s5_d.reshape(1, ch), w_glu, b_glu.reshape(1, ch)]
    return pl.pallas_call(
        _mix_kernel,
        grid=(g, nt),
        in_specs=[row_spec(d), row_spec(dl), row_spec(dl), row_spec(dl), s5_spec, s5_spec, s5_spec]
                 + [_const_spec(a.shape) for a in consts],
        out_specs=row_spec(d),
        out_shape=jax.ShapeDtypeStruct((g, n, d), F32),
        compiler_params=_params(("arbitrary", "arbitrary")),
        name="mix",
    )(xtm, ga, hf, hb, yf, yb, xs, *consts)


FF_CHUNKS = 11


def _ffn_kernel(x_ref, mod_ref, gpre_ref, gpost_ref, w1_ref, w3_ref, w2_ref, o_ref):
    rows, d = x_ref.shape
    tt = rows // SUBLANES
    x3 = x_ref[...].reshape(tt, SUBLANES, d)
    sh2 = mod_ref[:, 3 * d:4 * d][None]
    sc2 = mod_ref[:, 4 * d:5 * d][None]
    g2 = mod_ref[:, 5 * d:6 * d][None]
    hn = (_rms(x3) * gpre_ref[...] * (1.0 + sc2) + sh2).reshape(rows, d).astype(BF16)
    dff = w1_ref.shape[-1]
    cw = dff // FF_CHUNKS
    f = jnp.zeros((rows, d), F32)
    for k in range(FF_CHUNKS):
        cs = slice(k * cw, (k + 1) * cw)
        u1 = _dot(hn, w1_ref[:, cs])
        u3 = _dot(hn, w3_ref[:, cs])
        act = (u1 * jax.nn.sigmoid(u1) * u3).astype(BF16)
        f = f + _dot(act, w2_ref[cs, :])
    fn = (_rms(f) * gpost_ref[...]).reshape(tt, SUBLANES, d)
    o_ref[...] = jnp.swapaxes(x3 + g2 * fn, 0, 1)


def _ffn_call(x1, mod8, g_pre, g_post, w1, w3, w2):
    g, n, d = x1.shape
    tt = T_BLK
    rows = tt * SUBLANES
    nt = n // rows
    consts = [mod8, g_pre.reshape(1, d), g_post.reshape(1, d), w1, w3, w2]
    return pl.pallas_call(
        _ffn_kernel,
        grid=(g, nt),
        in_specs=[pl.BlockSpec((None, rows, d), lambda gi, ti: (gi, ti, 0))]
                 + [_const_spec(a.shape) for a in consts],
        out_specs=pl.BlockSpec((SUBLANES, tt, d), lambda gi, ti: (gi, ti, 0)),
        out_shape=jax.ShapeDtypeStruct((g * SUBLANES, nt * tt, d), F32),
        compiler_params=_params(("arbitrary", "arbitrary")),
        name="ffn",
    )(x1, *consts)


def _block_diag_tiles(w, per_tile):
    nb, k, m = w.shape
    w = w.reshape(nb // per_tile, per_tile, k, m)
    eye = jnp.eye(per_tile, dtype=w.dtype)
    t = jnp.einsum("tbkm,bc->tbkcm", w, eye)
    return t.reshape(nb // per_tile, per_tile * k, per_tile * m)


def _layer(x, mod8, p, h0_lru, h0_re, h0_im, colmajor):
    b, t, d = x.shape
    d_lru = p["conv_b"].shape[-1]
    d_s5 = p["s5_d"].shape[-1]
    xtm, xa, ga, xs = _stage1_call(x, mod8, p["g_pre_mix"], p["w_in"], d_lru, d_s5, colmajor)
    hs, fins = [], []
    ys, fr, fi = [], [], []
    hu = None
    for dr in range(2):
        res = _lru_call(xa if dr == 0 else hu, (p["conv_w"], p["conv_b"]) if dr == 0 else None, p["wr_bd"][dr],
                        p["wi_bd"][dr], p["lru_b_r"][dr], p["lru_b_i"][dr], p["lru_lambda"][dr], h0_lru[dr],
                        reverse=bool(dr))
        if dr == 0:
            hu = res[2]
        hs.append(res[0])
        fins.append(res[1])
        y, r_, i_ = _s5_call(xs, *p["s5_ops"][dr], h0_re[dr], h0_im[dr], reverse=bool(dr), colmajor=colmajor)
        ys.append(y)
        fr.append(r_)
        fi.append(i_)
    x1 = _mix_call(xtm, ga, hs[0], hs[1], ys[0], ys[1], xs, mod8, p["g_pre_mix"], p["g_post_mix"], p["w_gate"],
                   p["b_gate"], p["w_proj_lru"], p["w_proj_s5"], p["w_out"], p["s5_d"], p["s5_w_glu"],
                   p["s5_b_glu"], colmajor)
    out = _ffn_call(x1, mod8, p["g_pre_ffn"], p["g_post_ffn"], p["w_ff1"], p["w_ff3"], p["w_ff2"])
    return out, fins, fr, fi


def kernel(x_prompt, x_sample, c, state_lru, state_s5_re, state_s5_im, c_ctx, w_mod, b_mod, g_pre_mix, g_post_mix,
           g_pre_ffn, g_post_ffn, w_in, conv_w, conv_b, lru_w_r, lru_b_r, lru_w_i, lru_b_i, lru_lambda,
           s5_a_re, s5_a_im, s5_log_dt, s5_b_re, s5_b_im, s5_c_re, s5_c_im, s5_d, s5_w_glu, s5_b_glu,
           w_proj_lru, w_proj_s5, w_gate, b_gate, w_out, w_ff_in, w_ff_out):
    depth = w_mod.shape[0]
    n_ctx, _, d = x_prompt.shape
    n_dec = x_sample.shape[0]
    assert n_dec == SUBLANES and n_ctx % SUBLANES == 0
    g_ctx = n_ctx // SUBLANES
    n_dir, n_grp, n_st = s5_a_re.shape[1:]
    d_lru = conv_b.shape[-1]
    d_ff = w_ff_out.shape[1]
    heads_per_tile = MXU_TILE // lru_w_r.shape[-1]

    y_prompt, y_sample = x_prompt, x_sample
    lru_list, re_list, im_list = [], [], []
    for l in range(depth):
        cvecs = jnp.concatenate([c, c_ctx[None], jnp.zeros((SUBLANES - 1, d), F32)], axis=0)
        mod = _mod_call(cvecs, w_mod[l], b_mod[l])
        mod_dec = mod[0:SUBLANES]
        mod_ctx = jnp.broadcast_to(mod[SUBLANES:SUBLANES + 1], (SUBLANES, mod.shape[1]))

        p = {
            "g_pre_mix": g_pre_mix[l], "g_post_mix": g_post_mix[l], "g_pre_ffn": g_pre_ffn[l],
            "g_post_ffn": g_post_ffn[l], "w_in": w_in[l].astype(BF16), "conv_w": conv_w[l], "conv_b": conv_b[l],
            "wr_bd": [_block_diag_tiles(lru_w_r[l, dr], heads_per_tile).astype(BF16) for dr in range(n_dir)],
            "wi_bd": [_block_diag_tiles(lru_w_i[l, dr], heads_per_tile).astype(BF16) for dr in range(n_dir)],
            "lru_b_r": lru_b_r[l], "lru_b_i": lru_b_i[l],
            "lru_lambda": lru_lambda[l],
            "s5_ops": [_s5_chunk_operators(s5_a_re[l, dr], s5_a_im[l, dr], s5_log_dt[l, dr], s5_b_re[l, dr],
                                           s5_b_im[l, dr], s5_c_re[l, dr], s5_c_im[l, dr], reverse=bool(dr))
                       for dr in range(n_dir)],
            "s5_d": s5_d[l], "s5_w_glu": s5_w_glu[l].astype(BF16), "s5_b_glu": s5_b_glu[l],
            "w_proj_lru": w_proj_lru[l].astype(BF16), "w_proj_s5": w_proj_s5[l].astype(BF16),
            "w_gate": (0.5 * w_gate[l]).astype(BF16), "b_gate": b_gate[l], "w_out": (0.5 * w_out[l]).astype(BF16),
            "w_ff1": w_ff_in[l, :, :d_ff].astype(BF16), "w_ff3": w_ff_in[l, :, d_ff:].astype(BF16),
            "w_ff2": w_ff_out[l].astype(BF16),
        }
        ns = n_grp * n_st
        zero_lru = [jnp.zeros((g_ctx, SUBLANES, d_lru), F32)] * n_dir
        zero_s5 = [jnp.zeros((g_ctx, SUBLANES, ns), F32)] * n_dir
        y_prompt, f_lru, f_re, f_im = _layer(y_prompt, mod_ctx, p, zero_lru, zero_s5, zero_s5, colmajor=False)
        lru_list.append(jnp.stack([f.reshape(n_ctx, d_lru) for f in f_lru], axis=1))
        re_list.append(jnp.stack([f.reshape(n_ctx, n_grp, n_st) for f in f_re], axis=1))
        im_list.append(jnp.stack([f.reshape(n_ctx, n_grp, n_st) for f in f_im], axis=1))

        h0_lru = [state_lru[:, l, dr].astype(F32)[None] for dr in range(n_dir)]
        h0_re = [state_s5_re[:, l, dr].reshape(1, n_dec, ns) for dr in range(n_dir)]
        h0_im = [state_s5_im[:, l, dr].reshape(1, n_dec, ns) for dr in range(n_dir)]
        y_sample, _, _, _ = _layer(y_sample, mod_dec, p, h0_lru, h0_re, h0_im, colmajor=True)
    return (y_prompt, y_sample, jnp.stack(lru_list, axis=1), jnp.stack(re_list, axis=1),
            jnp.stack(im_list, axis=1))
```

```python
import functools

import jax
import jax.numpy as jnp
from jax import lax
from jax.experimental import pallas as pl
from jax.experimental.pallas import tpu as pltpu

F32 = jnp.float32
BF16 = jnp.bfloat16

EPS = 1e-6
LRU_C = 8.0
LOG2_E = 1.4426950408889634
GRID_W = 64
SUBLANES = 8
MXU_TILE = 256
S5_GROUP = 16
S5_STATE = 64
VMEM_LIMIT = 56 * 1024 * 1024

T_BLK = 64
LRU_T_BLK = 128
STAGE1_T_BLK = 128


def _const_spec(shape):
    nd = len(shape)
    return pl.BlockSpec(shape, lambda *_: (0,) * nd, pipeline_mode=pl.Buffered(1))


def _params(sem):
    return pltpu.CompilerParams(dimension_semantics=sem, vmem_limit_bytes=VMEM_LIMIT)


def _dot(a, b):
    return jnp.dot(a, b, preferred_element_type=F32)


def _rms(x):
    return x * lax.rsqrt(jnp.mean(x * x, axis=-1, keepdims=True) + EPS)


def _mod_kernel(c_ref, w_ref, b_ref, o_ref):
    c = c_ref[...]
    s = c * jax.nn.sigmoid(c)
    o_ref[...] = jnp.dot(s, w_ref[...], preferred_element_type=F32,
                         precision=lax.Precision.HIGHEST) + b_ref[...]


def _mod_call(cvecs, w_mod, b_mod):
    rows, d = cvecs.shape
    n = w_mod.shape[1]
    nb = 4
    return pl.pallas_call(
        _mod_kernel,
        grid=(nb,),
        in_specs=[pl.BlockSpec((rows, d), lambda j: (0, 0)),
                  pl.BlockSpec((d, n // nb), lambda j: (0, j)),
                  pl.BlockSpec((1, n // nb), lambda j: (0, j))],
        out_specs=pl.BlockSpec((rows, n // nb), lambda j: (0, j)),
        out_shape=jax.ShapeDtypeStruct((rows, n), F32),
        compiler_params=_params(("arbitrary",)),
        name="mod",
    )(cvecs, w_mod, b_mod.reshape(1, n))


def _stage1_kernel(x_ref, mod_ref, g_ref, w_ref, xtm_ref, xa_ref, ga_ref, xs_ref):
    _, tt, d = x_ref.shape
    rows = tt * SUBLANES
    x = jnp.swapaxes(x_ref[...], 0, 1)
    sh1 = mod_ref[:, 0:d][None]
    sc1 = mod_ref[:, d:2 * d][None]
    hn = _rms(x) * g_ref[...] * (1.0 + sc1) + sh1
    xtm_ref[...] = x.reshape(rows, d)
    z = _dot(hn.reshape(rows, d).astype(BF16), w_ref[...])
    dl = xa_ref.shape[-1]
    xa_ref[...] = z[:, 0:dl].astype(BF16)
    ga_ref[...] = z[:, dl:2 * dl].astype(BF16)
    zs = z[:, 2 * dl:]
    xs_ref[...] = zs.astype(BF16).reshape(xs_ref.shape)


def _stage1_call(x, mod8, g_pre, w_in, d_lru, d_s5, colmajor):
    b, t, d = x.shape
    g = b // SUBLANES
    tt = STAGE1_T_BLK
    nt = t // tt
    rows = tt * SUBLANES
    row_spec = lambda c: pl.BlockSpec((None, rows, c), lambda gi, ti: (gi, ti, 0))
    if colmajor:
        assert g == 1 and tt % GRID_W == 0
        per = GRID_W * SUBLANES
        xs_shape = (t // GRID_W, per, d_s5)
        xs_spec = pl.BlockSpec((tt // GRID_W, per, d_s5), lambda gi, ti: (ti, 0, 0))
    else:
        xs_shape = (g, t * SUBLANES, d_s5)
        xs_spec = row_spec(d_s5)
    return pl.pallas_call(
        _stage1_kernel,
        grid=(g, nt),
        in_specs=[pl.BlockSpec((SUBLANES, tt, d), lambda gi, ti: (gi, ti, 0)),
                  _const_spec(mod8.shape), _const_spec((1, d)), _const_spec(w_in.shape)],
        out_specs=[row_spec(d), row_spec(d_lru), row_spec(d_lru), xs_spec],
        out_shape=[jax.ShapeDtypeStruct((g, t * SUBLANES, d), F32),
                   jax.ShapeDtypeStruct((g, t * SUBLANES, d_lru), BF16),
                   jax.ShapeDtypeStruct((g, t * SUBLANES, d_lru), BF16),
                   jax.ShapeDtypeStruct(xs_shape, BF16)],
        compiler_params=_params(("arbitrary", "arbitrary")),
        name="stage1",
    )(x, mod8, g_pre.reshape(1, d), w_in)


def _lru_kernel(*refs, reverse, nt, from_conv):
    if from_conv:
        (xm_ref, xp_ref, xn_ref, cw_ref, cb_ref, wr_ref, wi_ref, br_ref, bi_ref, lam_ref, h0_ref,
         h_ref, fin_ref, hu_out_ref, xe_scr, a_scr, b_scr, carry) = refs
        rows, c = xm_ref.shape
    else:
        (hu_ref, wr_ref, wi_ref, br_ref, bi_ref, lam_ref, h0_ref, h_ref, fin_ref, a_scr, b_scr, carry) = refs
        rows, c = hu_ref.shape
    tt = rows // SUBLANES
    i = pl.program_id(1)
    blk = (nt - 1 - i) if reverse else i

    @pl.when(i == 0)
    def _():
        carry[...] = h0_ref[...]

    if from_conv:
        halo = 2 * SUBLANES
        prev_ok = (blk > 0).astype(F32)
        next_ok = (blk < nt - 1).astype(F32)
        xe_scr[0:halo] = xp_ref[...].astype(F32) * prev_ok
        xe_scr[halo:halo + rows] = xm_ref[...].astype(F32)
        xe_scr[halo + rows:halo + rows + SUBLANES] = xn_ref[0:SUBLANES].astype(F32) * next_ok

    for j in range(c // MXU_TILE):
        cs = slice(j * MXU_TILE, (j + 1) * MXU_TILE)
        cc = (-0.5 * LRU_C * LOG2_E) * jax.nn.softplus(-lam_ref[:, cs])
        if from_conv:
            hu = 0.5 * cb_ref[:, cs]
            for k in range(4):
                hu = hu + (0.5 * cw_ref[k:k + 1, cs]) * xe_scr[k * SUBLANES:k * SUBLANES + rows, cs]
            hub = hu.astype(BF16)
            hu_out_ref[:, cs] = hub
        else:
            hub = hu_ref[:, cs]
            hu = hub.astype(F32)
        tr = jnp.tanh(_dot(hub, wr_ref[j]) + 0.5 * br_ref[:, cs])
        ti = jnp.tanh(_dot(hub, wi_ref[j]) + 0.5 * bi_ref[:, cs])
        a = jnp.exp2(cc * tr + cc)
        y = 1.0 - a * a
        a_scr[:, cs] = a
        b_scr[:, cs] = jnp.where(y > 0.0, y * lax.rsqrt(y), 0.0) * (hu * ti + hu)

    def step(s, h):
        t = (tt - 1 - s) if reverse else s
        sl = pl.ds(pl.multiple_of(t * SUBLANES, SUBLANES), SUBLANES)
        h = a_scr[sl, :] * h + b_scr[sl, :]
        b_scr[sl, :] = h
        return h

    h = lax.fori_loop(0, tt, step, carry[...], unroll=8)
    carry[...] = h
    fin_ref[...] = h
    h_ref[...] = b_scr[...].astype(BF16)


def _lru_call(x, conv, wr_bd, wi_bd, b_r, b_i, lam, h0, reverse):
    g, n, c = x.shape
    tt = LRU_T_BLK
    rows = tt * SUBLANES
    nt = n // rows
    blk = (lambda ti: nt - 1 - ti) if reverse else (lambda ti: ti)
    main = pl.BlockSpec((None, rows, c), lambda gi, ti: (gi, blk(ti), 0))
    state = pl.BlockSpec((None, SUBLANES, c), lambda gi, ti: (gi, 0, 0))
    vec = lambda: _const_spec((1, c))
    gate_specs = [_const_spec(wr_bd.shape), _const_spec(wi_bd.shape), vec(), vec(), vec(), state]
    gate_args = (wr_bd, wi_bd, b_r.reshape(1, c), b_i.reshape(1, c), lam.reshape(1, c), h0)
    out_specs = [main, state]
    out_shape = [jax.ShapeDtypeStruct((g, n, c), BF16), jax.ShapeDtypeStruct((g, SUBLANES, c), F32)]
    scratch = [pltpu.VMEM((rows, c), F32), pltpu.VMEM((rows, c), F32), pltpu.VMEM((SUBLANES, c), F32)]
    if conv is not None:
        hb = 2 * SUBLANES
        per = rows // hb
        nhb = n // hb
        prev = pl.BlockSpec((None, hb, c), lambda gi, ti: (gi, jnp.maximum(blk(ti) * per - 1, 0), 0))
        nxt = pl.BlockSpec((None, hb, c), lambda gi, ti: (gi, jnp.minimum((blk(ti) + 1) * per, nhb - 1), 0))
        in_specs = [main, prev, nxt, _const_spec((4, c)), vec()] + gate_specs
        args = (x, x, x, conv[0], conv[1].reshape(1, c)) + gate_args
        out_specs = out_specs + [main]
        out_shape = out_shape + [jax.ShapeDtypeStruct((g, n, c), BF16)]
        scratch = [pltpu.VMEM((rows + 4 * SUBLANES, c), F32)] + scratch
    else:
        in_specs = [main] + gate_specs
        args = (x,) + gate_args
    return pl.pallas_call(
        functools.partial(_lru_kernel, reverse=reverse, nt=nt, from_conv=conv is not None),
        grid=(g, nt),
        in_specs=in_specs,
        out_specs=out_specs,
        out_shape=out_shape,
        scratch_shapes=scratch,
        compiler_params=_params(("arbitrary", "arbitrary")),
        name="lru_bwd" if reverse else "lru_fwd",
    )(*args)


S5_CHUNK = 16
S5_T_BLK = 512
S5_PAIR_UNROLL = 16


def _s5_kernel(xs_ref, m_ref, p_ref, q_ref, ar_ref, ai_ref, h0r_ref, h0i_ref, y_ref, finr_ref, fini_ref,
               xy_scr, st_scr, hin_scr, cr, ci, *, reverse, colmajor):
    ch = xs_ref.shape[-1]
    if colmajor:
        nseq = 1
        nrow, ncol = xs_ref.shape[0], xs_ref.shape[1] // SUBLANES
        nrh = nrow // S5_CHUNK
        nchunk = ncol * nrh
        chunk_row = lambda sq, m: ((m % nrh) * ncol + m // nrh) * SUBLANES
    else:
        nseq, nchunk = xs_ref.shape[0], xs_ref.shape[1]
        chunk_row = lambda sq, m: (sq * nchunk + m) * SUBLANES
    n = nseq * nchunk * SUBLANES
    ngrp = ch // S5_GROUP
    npair = ngrp // 2
    kdim = S5_CHUNK * S5_GROUP
    ps = S5_STATE
    unroll = S5_PAIR_UNROLL
    i = pl.program_id(1)

    @pl.when(i == 0)
    def _():
        for j in range(npair):
            for sq in range(nseq):
                cr[j, sq] = h0r_ref[sq, :, 2 * ps * j:2 * ps * (j + 1)]
                ci[j, sq] = h0i_ref[sq, :, 2 * ps * j:2 * ps * (j + 1)]

    if colmajor:
        for tau in range(S5_CHUNK):
            x = jnp.concatenate([xs_ref[rh * S5_CHUNK + tau] for rh in range(nrh)], axis=0)
            xy_scr[:, tau] = x.T.reshape(ngrp, S5_GROUP, n)
    else:
        for tp in range(S5_CHUNK // 2):
            v = xs_ref[:, :, tp].astype(F32)
            for e in range(2):
                x = v[:, :, e * SUBLANES:(e + 1) * SUBLANES, :].reshape(n, ch).astype(BF16)
                xy_scr[:, 2 * tp + e] = x.T.reshape(ngrp, S5_GROUP, n)

    def pairs(jj, carry_):
        base = jj * unroll
        rs = [_dot(jnp.concatenate([m_ref[2 * base + e], p_ref[2 * base + e]], axis=0),
                   xy_scr[2 * base + e].reshape(kdim, n)) for e in range(2 * unroll)]
        for u in range(unroll):
            r0, r1 = rs[2 * u], rs[2 * u + 1]
            st = jnp.concatenate([r0[kdim:kdim + ps], r1[kdim:kdim + ps], r0[kdim + ps:], r1[kdim + ps:]], axis=0)
            st_scr[u] = st.astype(BF16).T.astype(F32)
        ars = [jnp.broadcast_to(ar_ref[base + u], (SUBLANES, 2 * ps)) for u in range(unroll)]
        ais = [jnp.broadcast_to(ai_ref[base + u], (SUBLANES, 2 * ps)) for u in range(unroll)]
        state = [[(cr[base + u, sq], ci[base + u, sq]) for sq in range(nseq)] for u in range(unroll)]
        for s in range(nchunk):
            m = (nchunk - 1 - s) if reverse else s
            for u in range(unroll):
                for sq in range(nseq):
                    row = chunk_row(sq, m)
                    rows = slice(row, row + SUBLANES)
                    hr, hi = state[u][sq]
                    hin_scr[u, rows, 0:2 * ps] = hr
                    hin_scr[u, rows, 2 * ps:4 * ps] = hi
                    sr = st_scr[u, rows, 0:2 * ps]
                    si = st_scr[u, rows, 2 * ps:4 * ps]
                    state[u][sq] = (ars[u] * hr - ais[u] * hi + sr, ars[u] * hi + ais[u] * hr + si)
        for u in range(unroll):
            for sq in range(nseq):
                cr[base + u, sq], ci[base + u, sq] = state[u][sq]
        hts = [hin_scr[u].astype(BF16).T for u in range(unroll)]
        for u in range(unroll):
            ht = hts[u]
            for e in range(2):
                het = jnp.concatenate([ht[e * ps:(e + 1) * ps], ht[(2 + e) * ps:(3 + e) * ps]], axis=0)
                y = rs[2 * u + e][0:kdim] + _dot(q_ref[2 * (base + u) + e], het)
                xy_scr[2 * (base + u) + e] = y.astype(BF16).reshape(S5_CHUNK, S5_GROUP, n)
        return carry_

    lax.fori_loop(0, npair // unroll, pairs, 0)

    for j in range(npair):
        for sq in range(nseq):
            finr_ref[sq, :, 2 * ps * j:2 * ps * (j + 1)] = cr[j, sq]
            fini_ref[sq, :, 2 * ps * j:2 * ps * (j + 1)] = ci[j, sq]
    if colmajor:
        per = ncol * SUBLANES
        for tau in range(S5_CHUNK):
            yt = xy_scr[:, tau].reshape(ch, n).T
            for rh in range(nrh):
                y_ref[rh * S5_CHUNK + tau] = yt[rh * per:(rh + 1) * per]
    else:
        for tp in range(S5_CHUNK // 2):
            pair = [xy_scr[:, 2 * tp + e].reshape(ch, n).T.astype(F32).reshape(nseq, nchunk, SUBLANES, ch)
                    for e in range(2)]
            y_ref[:, :, tp] = jnp.concatenate(pair, axis=2).astype(BF16)


def _s5_call(xs, m, p, q, a16_re, a16_im, h0_re, h0_im, reverse, colmajor):
    ch = xs.shape[-1]
    ngrp = ch // S5_GROUP
    ns = ngrp * S5_STATE
    if colmajor:
        nrow = xs.shape[0]
        g, nseq = 1, 1
        ncol = S5_T_BLK // nrow
        nt = xs.shape[1] // (ncol * SUBLANES)
        n = S5_T_BLK // S5_CHUNK * SUBLANES
        blk = (lambda ti: nt - 1 - ti) if reverse else (lambda ti: ti)
        arr = xs
        main = pl.BlockSpec((nrow, ncol * SUBLANES, ch), lambda gi, ti: (0, blk(ti), 0))
    else:
        g, t8, _ = xs.shape
        t = t8 // SUBLANES
        tt = min(S5_T_BLK, t)
        nt = t // tt
        nseq = min(g, S5_T_BLK // tt)
        nchunk = tt // S5_CHUNK
        n = nseq * nchunk * SUBLANES
        blk = (lambda ti: nt - 1 - ti) if reverse else (lambda ti: ti)
        arr = xs.reshape(g, t // S5_CHUNK, S5_CHUNK // 2, 2 * SUBLANES, ch)
        main = pl.BlockSpec((nseq, nchunk, S5_CHUNK // 2, 2 * SUBLANES, ch), lambda gi, ti: (gi, blk(ti), 0, 0, 0))
    st = pl.BlockSpec((nseq, SUBLANES, ns), lambda gi, ti: (gi, 0, 0))
    y, fr, fi = pl.pallas_call(
        functools.partial(_s5_kernel, reverse=reverse, colmajor=colmajor),
        grid=(g // nseq, nt),
        in_specs=[main, _const_spec(m.shape), _const_spec(p.shape), _const_spec(q.shape), _const_spec(a16_re.shape),
                  _const_spec(a16_im.shape), st, st],
        out_specs=[main, st, st],
        out_shape=[jax.ShapeDtypeStruct(arr.shape, BF16), jax.ShapeDtypeStruct((g, SUBLANES, ns), F32),
                   jax.ShapeDtypeStruct((g, SUBLANES, ns), F32)],
        scratch_shapes=[pltpu.VMEM((ngrp, S5_CHUNK, S5_GROUP, n), BF16),
                        pltpu.VMEM((S5_PAIR_UNROLL, n, 4 * S5_STATE), F32),
                        pltpu.VMEM((S5_PAIR_UNROLL, n, 4 * S5_STATE), F32),
                        pltpu.VMEM((ngrp // 2, nseq, SUBLANES, 2 * S5_STATE), F32),
                        pltpu.VMEM((ngrp // 2, nseq, SUBLANES, 2 * S5_STATE), F32)],
        compiler_params=_params(("arbitrary", "arbitrary")),
        name="s5_bwd" if reverse else "s5_fwd",
    )(arr, m, p, q, a16_re, a16_im, h0_re, h0_im)
    return y.reshape(xs.shape), fr, fi


def _toeplitz_kernel(lag_ref, m_ref, *, reverse):
    ng, hh, width = lag_ref.shape
    n = S5_CHUNK * hh
    q0 = S5_CHUNK if reverse else S5_CHUNK - 1
    for g in range(ng):
        x = lag_ref[g]
        for t in range(S5_CHUNK):
            start = (q0 - t) * hh
            win = x if start == 0 else pltpu.roll(x, width - start, axis=1)
            m_ref[g, t * hh:(t + 1) * hh, :] = win[:, 0:n].astype(BF16)


def _toeplitz_call(lags, reverse):
    ngrp, hh, width = lags.shape
    n = S5_CHUNK * hh
    gb = 8
    return pl.pallas_call(
        functools.partial(_toeplitz_kernel, reverse=reverse),
        grid=(ngrp // gb,),
        in_specs=[pl.BlockSpec((gb, hh, width), lambda i: (i, 0, 0))],
        out_specs=pl.BlockSpec((gb, n, n), lambda i: (i, 0, 0)),
        out_shape=jax.ShapeDtypeStruct((ngrp, n, n), BF16),
        compiler_params=_params(("arbitrary",)),
        name="s5_toeplitz",
    )(lags)


def _s5_chunk_operators(a_re, a_im, log_dt, b_re, b_im, c_re, c_im, reverse):
    hi_p = lax.Precision.HIGHEST
    L = S5_CHUNK
    ngrp, ps, hh = b_re.shape
    dt = jnp.exp(log_dt)[:, None]
    lr, li = a_re * dt, a_im * dt

    def powers(e, xr, xi):
        mag = jnp.exp(e * xr)
        return mag * jnp.cos(e * xi), mag * jnp.sin(e * xi)

    abar_re, abar_im = powers(1.0, lr, li)
    den = a_re * a_re + a_im * a_im
    nr, ni = abar_re - 1.0, abar_im
    f_re = (nr * a_re + ni * a_im) / den
    f_im = (ni * a_re - nr * a_im) / den
    bb_re = f_re[..., None] * b_re - f_im[..., None] * b_im
    bb_im = f_re[..., None] * b_im + f_im[..., None] * b_re

    steps = jnp.arange(L, dtype=F32)
    lag_k = steps if reverse else (L - 1.0 - steps)
    pk_re, pk_im = powers(lag_k[None, :, None], lr[:, None, :], li[:, None, :])
    d_re = pk_re[..., None] * bb_re[:, None] - pk_im[..., None] * bb_im[:, None]
    d_im = pk_re[..., None] * bb_im[:, None] + pk_im[..., None] * bb_re[:, None]
    kk = jnp.einsum("gip,gkpj->gikj", jnp.concatenate([c_re, -c_im], axis=-1),
                    jnp.concatenate([d_re, d_im], axis=2), precision=hi_p).reshape(ngrp, hh, L * hh)
    zeros = jnp.zeros_like(kk)
    lags = jnp.concatenate([zeros, kk] if reverse else [kk, zeros], axis=-1)
    m = _toeplitz_call(lags, reverse)

    lr2 = jnp.concatenate([lr, lr], axis=-1)
    li2 = jnp.concatenate([li, li], axis=-1)
    e_p = steps if reverse else (L - 1.0 - steps)
    pr2, pi2 = powers(e_p[None, None, :], lr2[:, :, None], li2[:, :, None])
    pa = jnp.concatenate([bb_re, bb_im], axis=1)
    pb = jnp.concatenate([-bb_im, bb_re], axis=1)
    p = (pr2[..., None] * pa[:, :, None, :] + pi2[..., None] * pb[:, :, None, :]).reshape(ngrp, 2 * ps, L * hh)

    e_q = (L - steps) if reverse else (steps + 1.0)
    qr2, qi2 = powers(e_q[None, :, None], lr2[:, None, :], li2[:, None, :])
    qa = jnp.concatenate([c_re, -c_im], axis=-1)
    qb = jnp.concatenate([-c_im, -c_re], axis=-1)
    q = (qa[:, None] * qr2[:, :, None, :] + qb[:, None] * qi2[:, :, None, :]).reshape(ngrp, L * hh, 2 * ps)

    a16_re, a16_im = powers(float(L), lr, li)
    return (m, p.astype(BF16), q.astype(BF16), a16_re.reshape(ngrp // 2, 1, 2 * ps),
            a16_im.reshape(ngrp // 2, 1, 2 * ps))


def _mix_kernel(x_ref, ga_ref, hf_ref, hb_ref, yf_ref, yb_ref, xs_ref, mod_ref, gpre_ref, gpost_ref,
                wg_ref, bg_ref, wpl_ref, wps_ref, wo_ref, d_ref, wglu_ref, bglu_ref, o_ref):
    rows, d = x_ref.shape
    tt = rows // SUBLANES
    x3 = x_ref[...].reshape(tt, SUBLANES, d)
    sh1 = mod_ref[:, 0:d][None]
    sc1 = mod_ref[:, d:2 * d][None]
    g1 = mod_ref[:, 2 * d:3 * d][None]
    hn = (_rms(x3) * gpre_ref[...] * (1.0 + sc1) + sh1).reshape(rows, d).astype(BF16)
    zg = _dot(hn, wg_ref[...])

    ha = hf_ref[...].astype(F32) + hb_ref[...].astype(F32)
    ya = jax.nn.gelu(ga_ref[...].astype(F32)) * ha
    pa = _dot(ya.astype(BF16), wpl_ref[...])

    us = xs_ref[...].astype(F32)
    hs = yf_ref[...].astype(F32) + yb_ref[...].astype(F32)
    vs = jax.nn.gelu(hs + d_ref[...] * us)
    zglu = _dot(vs.astype(BF16), wglu_ref[...])
    tg = jnp.tanh(zg + bg_ref[...])
    ys = vs * jax.nn.sigmoid(zglu + bglu_ref[...])
    pb = _dot(ys.astype(BF16), wps_ref[...])

    mm = (pa + tg[:, 0:d] * pa) + (pb + tg[:, d:2 * d] * pb)
    m = _dot(mm.astype(BF16), wo_ref[...])
    mn = (_rms(m) * gpost_ref[...]).reshape(tt, SUBLANES, d)
    o_ref[...] = (x3 + g1 * mn).reshape(rows, d)


def _mix_call(xtm, ga, hf, hb, yf, yb, xs, mod8, g_pre, g_post, w_gate, b_gate, w_pl, w_ps, w_out,
              s5_d, w_glu, b_glu, colmajor):
    g, n, d = xtm.shape
    tt = T_BLK
    rows = tt * SUBLANES
    nt = n // rows
    ch = xs.shape[-1]
    row_spec = lambda c: pl.BlockSpec((None, rows, c), lambda gi, ti: (gi, ti, 0))
    if colmajor:
        assert tt == GRID_W
        s5_spec = pl.BlockSpec((None, rows, ch), lambda gi, ti: (ti, 0, 0))
    else:
        s5_spec = row_spec(ch)
    dl = ga.shape[-1]
    consts = [mod8, g_pre.reshape(1, d), g_post.reshape(1, d), w_gate, 0.5 * b_gate.reshape(1, 2 * d), w_pl, w_ps,
              w_out, s5_d.reshape(1, ch), w_glu, b_glu.reshape(1, ch)]
    return pl.pallas_call(
        _mix_kernel,
        grid=(g, nt),
        in_specs=[row_spec(d), row_spec(dl), row_spec(dl), row_spec(dl), s5_spec, s5_spec, s5_spec]
                 + [_const_spec(a.shape) for a in consts],
        out_specs=row_spec(d),
        out_shape=jax.ShapeDtypeStruct((g, n, d), F32),
        compiler_params=_params(("arbitrary", "arbitrary")),
        name="mix",
    )(xtm, ga, hf, hb, yf, yb, xs, *consts)


FF_CHUNKS = 11


def _ffn_kernel(x_ref, mod_ref, gpre_ref, gpost_ref, w13_ref, w2_ref, o_ref):
    rows, d = x_ref.shape
    tt = rows // SUBLANES
    x3 = x_ref[...].reshape(tt, SUBLANES, d)
    sh2 = mod_ref[:, 3 * d:4 * d][None]
    sc2 = mod_ref[:, 4 * d:5 * d][None]
    g2 = mod_ref[:, 5 * d:6 * d][None]
    hn = (_rms(x3) * gpre_ref[...] * (1.0 + sc2) + sh2).reshape(rows, d).astype(BF16)
    dff = w2_ref.shape[0]
    cw = dff // FF_CHUNKS
    f = jnp.zeros((rows, d), F32)
    for k in range(FF_CHUNKS):
        cs = slice(k * cw, (k + 1) * cw)
        u1 = _dot(hn, w13_ref[:, cs])
        u3 = _dot(hn, w13_ref[:, dff + k * cw:dff + (k + 1) * cw])
        act = (u1 * jax.nn.sigmoid(u1) * u3).astype(BF16)
        f = f + _dot(act, w2_ref[cs, :])
    fn = (_rms(f) * gpost_ref[...]).reshape(tt, SUBLANES, d)
    o_ref[...] = jnp.swapaxes(x3 + g2 * fn, 0, 1)


def _ffn_call(x1, mod8, g_pre, g_post, w13, w2):
    g, n, d = x1.shape
    tt = T_BLK
    rows = tt * SUBLANES
    nt = n // rows
    consts = [mod8, g_pre.reshape(1, d), g_post.reshape(1, d), w13, w2]
    return pl.pallas_call(
        _ffn_kernel,
        grid=(g, nt),
        in_specs=[pl.BlockSpec((None, rows, d), lambda gi, ti: (gi, ti, 0))]
                 + [_const_spec(a.shape) for a in consts],
        out_specs=pl.BlockSpec((SUBLANES, tt, d), lambda gi, ti: (gi, ti, 0)),
        out_shape=jax.ShapeDtypeStruct((g * SUBLANES, nt * tt, d), F32),
        compiler_params=_params(("arbitrary", "arbitrary")),
        name="ffn",
    )(x1, *consts)


def _block_diag_tiles(w, per_tile):
    nb, k, m = w.shape
    w = w.reshape(nb // per_tile, per_tile, k, m)
    eye = jnp.eye(per_tile, dtype=w.dtype)
    t = jnp.einsum("tbkm,bc->tbkcm", w, eye)
    return t.reshape(nb // per_tile, per_tile * k, per_tile * m)


def _layer(x, mod8, p, h0_lru, h0_re, h0_im, colmajor):
    b, t, d = x.shape
    d_lru = p["conv_b"].shape[-1]
    d_s5 = p["s5_d"].shape[-1]
    xtm, xa, ga, xs = _stage1_call(x, mod8, p["g_pre_mix"], p["w_in"], d_lru, d_s5, colmajor)
    hs, fins = [], []
    ys, fr, fi = [], [], []
    hu = None
    for dr in range(2):
        res = _lru_call(xa if dr == 0 else hu, (p["conv_w"], p["conv_b"]) if dr == 0 else None, p["wr_bd"][dr],
                        p["wi_bd"][dr], p["lru_b_r"][dr], p["lru_b_i"][dr], p["lru_lambda"][dr], h0_lru[dr],
                        reverse=bool(dr))
        if dr == 0:
            hu = res[2]
        hs.append(res[0])
        fins.append(res[1])
        y, r_, i_ = _s5_call(xs, *p["s5_ops"][dr], h0_re[dr], h0_im[dr], reverse=bool(dr), colmajor=colmajor)
        ys.append(y)
        fr.append(r_)
        fi.append(i_)
    x1 = _mix_call(xtm, ga, hs[0], hs[1], ys[0], ys[1], xs, mod8, p["g_pre_mix"], p["g_post_mix"], p["w_gate"],
                   p["b_gate"], p["w_proj_lru"], p["w_proj_s5"], p["w_out"], p["s5_d"], p["s5_w_glu"],
                   p["s5_b_glu"], colmajor)
    out = _ffn_call(x1, mod8, p["g_pre_ffn"], p["g_post_ffn"], p["w_ff13"], p["w_ff2"])
    return out, fins, fr, fi


def kernel(x_prompt, x_sample, c, state_lru, state_s5_re, state_s5_im, c_ctx, w_mod, b_mod, g_pre_mix, g_post_mix,
           g_pre_ffn, g_post_ffn, w_in, conv_w, conv_b, lru_w_r, lru_b_r, lru_w_i, lru_b_i, lru_lambda,
           s5_a_re, s5_a_im, s5_log_dt, s5_b_re, s5_b_im, s5_c_re, s5_c_im, s5_d, s5_w_glu, s5_b_glu,
           w_proj_lru, w_proj_s5, w_gate, b_gate, w_out, w_ff_in, w_ff_out):
    depth = w_mod.shape[0]
    n_ctx, _, d = x_prompt.shape
    n_dec = x_sample.shape[0]
    assert n_dec == SUBLANES and n_ctx % SUBLANES == 0
    g_ctx = n_ctx // SUBLANES
    n_dir, n_grp, n_st = s5_a_re.shape[1:]
    d_lru = conv_b.shape[-1]
    heads_per_tile = MXU_TILE // lru_w_r.shape[-1]

    y_prompt, y_sample = x_prompt, x_sample
    lru_list, re_list, im_list = [], [], []
    for l in range(depth):
        cvecs = jnp.concatenate([c, c_ctx[None], jnp.zeros((SUBLANES - 1, d), F32)], axis=0)
        mod = _mod_call(cvecs, w_mod[l], b_mod[l])
        mod_dec = mod[0:SUBLANES]
        mod_ctx = jnp.broadcast_to(mod[SUBLANES:SUBLANES + 1], (SUBLANES, mod.shape[1]))

        p = {
            "g_pre_mix": g_pre_mix[l], "g_post_mix": g_post_mix[l], "g_pre_ffn": g_pre_ffn[l],
            "g_post_ffn": g_post_ffn[l], "w_in": w_in[l].astype(BF16), "conv_w": conv_w[l], "conv_b": conv_b[l],
            "wr_bd": [_block_diag_tiles(lru_w_r[l, dr], heads_per_tile).astype(BF16) for dr in range(n_dir)],
            "wi_bd": [_block_diag_tiles(lru_w_i[l, dr], heads_per_tile).astype(BF16) for dr in range(n_dir)],
            "lru_b_r": lru_b_r[l], "lru_b_i": lru_b_i[l],
            "lru_lambda": lru_lambda[l],
            "s5_ops": [_s5_chunk_operators(s5_a_re[l, dr], s5_a_im[l, dr], s5_log_dt[l, dr], s5_b_re[l, dr],
                                           s5_b_im[l, dr], s5_c_re[l, dr], s5_c_im[l, dr], reverse=bool(dr))
                       for dr in range(n_dir)],
            "s5_d": s5_d[l], "s5_w_glu": s5_w_glu[l].astype(BF16), "s5_b_glu": s5_b_glu[l],
            "w_proj_lru": w_proj_lru[l].astype(BF16), "w_proj_s5": w_proj_s5[l].astype(BF16),
            "w_gate": (0.5 * w_gate[l]).astype(BF16), "b_gate": b_gate[l], "w_out": (0.5 * w_out[l]).astype(BF16),
            "w_ff13": w_ff_in[l].astype(BF16), "w_ff2": w_ff_out[l].astype(BF16),
        }
        ns = n_grp * n_st
        zero_lru = [jnp.zeros((g_ctx, SUBLANES, d_lru), F32)] * n_dir
        zero_s5 = [jnp.zeros((g_ctx, SUBLANES, ns), F32)] * n_dir
        y_prompt, f_lru, f_re, f_im = _layer(y_prompt, mod_ctx, p, zero_lru, zero_s5, zero_s5, colmajor=False)
        lru_list.append(jnp.stack([f.reshape(n_ctx, d_lru) for f in f_lru], axis=1))
        re_list.append(jnp.stack([f.reshape(n_ctx, n_grp, n_st) for f in f_re], axis=1))
        im_list.append(jnp.stack([f.reshape(n_ctx, n_grp, n_st) for f in f_im], axis=1))

        h0_lru = [state_lru[:, l, dr].astype(F32)[None] for dr in range(n_dir)]
        h0_re = [state_s5_re[:, l, dr].reshape(1, n_dec, ns) for dr in range(n_dir)]
        h0_im = [state_s5_im[:, l, dr].reshape(1, n_dec, ns) for dr in range(n_dir)]
        y_sample, _, _, _ = _layer(y_sample, mod_dec, p, h0_lru, h0_re, h0_im, colmajor=True)
    return (y_prompt, y_sample, jnp.stack(lru_list, axis=1), jnp.stack(re_list, axis=1),
            jnp.stack(im_list, axis=1))
```

```python
import functools

import jax
import jax.numpy as jnp
from jax import lax
from jax.experimental import pallas as pl
from jax.experimental.pallas import tpu as pltpu

F32 = jnp.float32
BF16 = jnp.bfloat16

EPS = 1e-6
LRU_C = 8.0
LOG2_E = 1.4426950408889634
GRID_W = 64
SUBLANES = 8
MXU_TILE = 256
S5_GROUP = 16
S5_STATE = 64
VMEM_LIMIT = 56 * 1024 * 1024

T_BLK = 64
LRU_T_BLK = 128
STAGE1_T_BLK = 128


def _const_spec(shape):
    nd = len(shape)
    return pl.BlockSpec(shape, lambda *_: (0,) * nd, pipeline_mode=pl.Buffered(1))


def _params(sem):
    return pltpu.CompilerParams(dimension_semantics=sem, vmem_limit_bytes=VMEM_LIMIT)


def _dot(a, b):
    return jnp.dot(a, b, preferred_element_type=F32)


def _rms(x):
    return x * lax.rsqrt(jnp.mean(x * x, axis=-1, keepdims=True) + EPS)


def _mod_kernel(c_ref, w_ref, b_ref, o_ref):
    c = c_ref[...]
    s = c * jax.nn.sigmoid(c)
    o_ref[...] = jnp.dot(s, w_ref[...], preferred_element_type=F32,
                         precision=lax.Precision.HIGHEST) + b_ref[...]


def _mod_call(cvecs, w_mod, b_mod):
    rows, d = cvecs.shape
    n = w_mod.shape[1]
    nb = 4
    return pl.pallas_call(
        _mod_kernel,
        grid=(nb,),
        in_specs=[pl.BlockSpec((rows, d), lambda j: (0, 0)),
                  pl.BlockSpec((d, n // nb), lambda j: (0, j)),
                  pl.BlockSpec((1, n // nb), lambda j: (0, j))],
        out_specs=pl.BlockSpec((rows, n // nb), lambda j: (0, j)),
        out_shape=jax.ShapeDtypeStruct((rows, n), F32),
        compiler_params=_params(("arbitrary",)),
        name="mod",
    )(cvecs, w_mod, b_mod.reshape(1, n))


def _stage1_kernel(x_ref, mod_ref, g_ref, w_ref, xtm_ref, xa_ref, ga_ref, xs_ref):
    _, tt, d = x_ref.shape
    rows = tt * SUBLANES
    x = jnp.swapaxes(x_ref[...], 0, 1)
    sh1 = mod_ref[:, 0:d][None]
    sc1 = mod_ref[:, d:2 * d][None]
    hn = _rms(x) * g_ref[...] * (1.0 + sc1) + sh1
    xtm_ref[...] = x.reshape(rows, d)
    z = _dot(hn.reshape(rows, d).astype(BF16), w_ref[...])
    dl = xa_ref.shape[-1]
    xa_ref[...] = z[:, 0:dl].astype(BF16)
    ga_ref[...] = z[:, dl:2 * dl].astype(BF16)
    zs = z[:, 2 * dl:]
    xs_ref[...] = zs.astype(BF16).reshape(xs_ref.shape)


def _stage1_call(x, mod8, g_pre, w_in, d_lru, d_s5, colmajor):
    b, t, d = x.shape
    g = b // SUBLANES
    tt = STAGE1_T_BLK
    nt = t // tt
    rows = tt * SUBLANES
    row_spec = lambda c: pl.BlockSpec((None, rows, c), lambda gi, ti: (gi, ti, 0))
    if colmajor:
        assert g == 1 and tt % GRID_W == 0
        per = GRID_W * SUBLANES
        xs_shape = (t // GRID_W, per, d_s5)
        xs_spec = pl.BlockSpec((tt // GRID_W, per, d_s5), lambda gi, ti: (ti, 0, 0))
    else:
        xs_shape = (g, t * SUBLANES, d_s5)
        xs_spec = row_spec(d_s5)
    return pl.pallas_call(
        _stage1_kernel,
        grid=(g, nt),
        in_specs=[pl.BlockSpec((SUBLANES, tt, d), lambda gi, ti: (gi, ti, 0)),
                  _const_spec(mod8.shape), _const_spec((1, d)), _const_spec(w_in.shape)],
        out_specs=[row_spec(d), row_spec(d_lru), row_spec(d_lru), xs_spec],
        out_shape=[jax.ShapeDtypeStruct((g, t * SUBLANES, d), F32),
                   jax.ShapeDtypeStruct((g, t * SUBLANES, d_lru), BF16),
                   jax.ShapeDtypeStruct((g, t * SUBLANES, d_lru), BF16),
                   jax.ShapeDtypeStruct(xs_shape, BF16)],
        compiler_params=_params(("arbitrary", "arbitrary")),
        name="stage1",
    )(x, mod8, g_pre.reshape(1, d), w_in)


def _lru_kernel(*refs, reverse, nt, from_conv):
    if from_conv:
        (xm_ref, xp_ref, xn_ref, cw_ref, cb_ref, wr_ref, wi_ref, br_ref, bi_ref, lam_ref, h0_ref,
         h_ref, fin_ref, hu_out_ref, xe_scr, a_scr, b_scr, carry) = refs
        rows, c = xm_ref.shape
    else:
        (hu_ref, wr_ref, wi_ref, br_ref, bi_ref, lam_ref, h0_ref, h_ref, fin_ref, a_scr, b_scr, carry) = refs
        rows, c = hu_ref.shape
    tt = rows // SUBLANES
    i = pl.program_id(1)
    blk = (nt - 1 - i) if reverse else i

    @pl.when(i == 0)
    def _():
        carry[...] = h0_ref[...]

    if from_conv:
        halo = 2 * SUBLANES
        prev_ok = (blk > 0).astype(F32)
        next_ok = (blk < nt - 1).astype(F32)
        xe_scr[0:halo] = xp_ref[...].astype(F32) * prev_ok
        xe_scr[halo:halo + rows] = xm_ref[...].astype(F32)
        xe_scr[halo + rows:halo + rows + SUBLANES] = xn_ref[0:SUBLANES].astype(F32) * next_ok

    for j in range(c // MXU_TILE):
        cs = slice(j * MXU_TILE, (j + 1) * MXU_TILE)
        cc = (-0.5 * LRU_C * LOG2_E) * jax.nn.softplus(-lam_ref[:, cs])
        if from_conv:
            hu = 0.5 * cb_ref[:, cs]
            for k in range(4):
                hu = hu + (0.5 * cw_ref[k:k + 1, cs]) * xe_scr[k * SUBLANES:k * SUBLANES + rows, cs]
            hub = hu.astype(BF16)
            hu_out_ref[:, cs] = hub
        else:
            hub = hu_ref[:, cs]
            hu = hub.astype(F32)
        tr = jnp.tanh(_dot(hub, wr_ref[j]) + 0.5 * br_ref[:, cs])
        ti = jnp.tanh(_dot(hub, wi_ref[j]) + 0.5 * bi_ref[:, cs])
        a = jnp.exp2(cc * tr + cc)
        y = 1.0 - a * a
        a_scr[:, cs] = a
        b_scr[:, cs] = jnp.where(y > 0.0, y * lax.rsqrt(y), 0.0) * (hu * ti + hu)

    def step(s, h):
        t = (tt - 1 - s) if reverse else s
        sl = pl.ds(pl.multiple_of(t * SUBLANES, SUBLANES), SUBLANES)
        h = a_scr[sl, :] * h + b_scr[sl, :]
        b_scr[sl, :] = h
        return h

    h = lax.fori_loop(0, tt, step, carry[...], unroll=8)
    carry[...] = h
    fin_ref[...] = h
    h_ref[...] = b_scr[...].astype(BF16)


def _lru_call(x, conv, wr_bd, wi_bd, b_r, b_i, lam, h0, reverse):
    g, n, c = x.shape
    tt = LRU_T_BLK
    rows = tt * SUBLANES
    nt = n // rows
    blk = (lambda ti: nt - 1 - ti) if reverse else (lambda ti: ti)
    main = pl.BlockSpec((None, rows, c), lambda gi, ti: (gi, blk(ti), 0))
    state = pl.BlockSpec((None, SUBLANES, c), lambda gi, ti: (gi, 0, 0))
    vec = lambda: _const_spec((1, c))
    gate_specs = [_const_spec(wr_bd.shape), _const_spec(wi_bd.shape), vec(), vec(), vec(), state]
    gate_args = (wr_bd, wi_bd, b_r.reshape(1, c), b_i.reshape(1, c), lam.reshape(1, c), h0)
    out_specs = [main, state]
    out_shape = [jax.ShapeDtypeStruct((g, n, c), BF16), jax.ShapeDtypeStruct((g, SUBLANES, c), F32)]
    scratch = [pltpu.VMEM((rows, c), F32), pltpu.VMEM((rows, c), F32), pltpu.VMEM((SUBLANES, c), F32)]
    if conv is not None:
        hb = 2 * SUBLANES
        per = rows // hb
        nhb = n // hb
        prev = pl.BlockSpec((None, hb, c), lambda gi, ti: (gi, jnp.maximum(blk(ti) * per - 1, 0), 0))
        nxt = pl.BlockSpec((None, hb, c), lambda gi, ti: (gi, jnp.minimum((blk(ti) + 1) * per, nhb - 1), 0))
        in_specs = [main, prev, nxt, _const_spec((4, c)), vec()] + gate_specs
        args = (x, x, x, conv[0], conv[1].reshape(1, c)) + gate_args
        out_specs = out_specs + [main]
        out_shape = out_shape + [jax.ShapeDtypeStruct((g, n, c), BF16)]
        scratch = [pltpu.VMEM((rows + 4 * SUBLANES, c), F32)] + scratch
    else:
        in_specs = [main] + gate_specs
        args = (x,) + gate_args
    return pl.pallas_call(
        functools.partial(_lru_kernel, reverse=reverse, nt=nt, from_conv=conv is not None),
        grid=(g, nt),
        in_specs=in_specs,
        out_specs=out_specs,
        out_shape=out_shape,
        scratch_shapes=scratch,
        compiler_params=_params(("arbitrary", "arbitrary")),
        name="lru_bwd" if reverse else "lru_fwd",
    )(*args)


S5_CHUNK = 16
S5_T_BLK = 512
S5_PAIR_UNROLL = 16


def _s5_kernel(xs_ref, m_ref, p_ref, q_ref, ar_ref, ai_ref, h0r_ref, h0i_ref, y_ref, finr_ref, fini_ref,
               xy_scr, st_scr, hin_scr, cr, ci, *, reverse, colmajor):
    ch = xs_ref.shape[-1]
    if colmajor:
        nseq = 1
        nrow, ncol = xs_ref.shape[0], xs_ref.shape[1] // SUBLANES
        nrh = nrow // S5_CHUNK
        nchunk = ncol * nrh
        chunk_row = lambda sq, m: ((m % nrh) * ncol + m // nrh) * SUBLANES
    else:
        nseq, nchunk = xs_ref.shape[0], xs_ref.shape[1]
        chunk_row = lambda sq, m: (sq * nchunk + m) * SUBLANES
    n = nseq * nchunk * SUBLANES
    ngrp = ch // S5_GROUP
    npair = ngrp // 2
    kdim = S5_CHUNK * S5_GROUP
    ps = S5_STATE
    unroll = S5_PAIR_UNROLL
    i = pl.program_id(1)

    @pl.when(i == 0)
    def _():
        for j in range(npair):
            for sq in range(nseq):
                cr[j, sq] = h0r_ref[sq, :, 2 * ps * j:2 * ps * (j + 1)]
                ci[j, sq] = h0i_ref[sq, :, 2 * ps * j:2 * ps * (j + 1)]

    if colmajor:
        for tau in range(S5_CHUNK):
            x = jnp.concatenate([xs_ref[rh * S5_CHUNK + tau] for rh in range(nrh)], axis=0)
            xy_scr[:, tau] = x.T.reshape(ngrp, S5_GROUP, n)
    else:
        for tp in range(S5_CHUNK // 2):
            v = xs_ref[:, :, tp].astype(F32)
            for e in range(2):
                x = v[:, :, e * SUBLANES:(e + 1) * SUBLANES, :].reshape(n, ch).astype(BF16)
                xy_scr[:, 2 * tp + e] = x.T.reshape(ngrp, S5_GROUP, n)

    def pairs(jj, carry_):
        base = jj * unroll
        rs = [_dot(jnp.concatenate([m_ref[2 * base + e], p_ref[2 * base + e]], axis=0),
                   xy_scr[2 * base + e].reshape(kdim, n)) for e in range(2 * unroll)]
        for u in range(unroll):
            r0, r1 = rs[2 * u], rs[2 * u + 1]
            st = jnp.concatenate([r0[kdim:kdim + ps], r1[kdim:kdim + ps], r0[kdim + ps:], r1[kdim + ps:]], axis=0)
            st_scr[u] = st.astype(BF16).T.astype(F32)
        ars = [jnp.broadcast_to(ar_ref[base + u], (SUBLANES, 2 * ps)) for u in range(unroll)]
        ais = [jnp.broadcast_to(ai_ref[base + u], (SUBLANES, 2 * ps)) for u in range(unroll)]
        state = [[(cr[base + u, sq], ci[base + u, sq]) for sq in range(nseq)] for u in range(unroll)]
        for s in range(nchunk):
            m = (nchunk - 1 - s) if reverse else s
            for u in range(unroll):
                for sq in range(nseq):
                    row = chunk_row(sq, m)
                    rows = slice(row, row + SUBLANES)
                    hr, hi = state[u][sq]
                    hin_scr[u, rows, 0:2 * ps] = hr
                    hin_scr[u, rows, 2 * ps:4 * ps] = hi
                    sr = st_scr[u, rows, 0:2 * ps]
                    si = st_scr[u, rows, 2 * ps:4 * ps]
                    state[u][sq] = (ars[u] * hr - ais[u] * hi + sr, ars[u] * hi + ais[u] * hr + si)
        for u in range(unroll):
            for sq in range(nseq):
                cr[base + u, sq], ci[base + u, sq] = state[u][sq]
        hts = [hin_scr[u].astype(BF16).T for u in range(unroll)]
        for u in range(unroll):
            ht = hts[u]
            for e in range(2):
                het = jnp.concatenate([ht[e * ps:(e + 1) * ps], ht[(2 + e) * ps:(3 + e) * ps]], axis=0)
                y = rs[2 * u + e][0:kdim] + _dot(q_ref[2 * (base + u) + e], het)
                xy_scr[2 * (base + u) + e] = y.astype(BF16).reshape(S5_CHUNK, S5_GROUP, n)
        return carry_

    lax.fori_loop(0, npair // unroll, pairs, 0)

    for j in range(npair):
        for sq in range(nseq):
            finr_ref[sq, :, 2 * ps * j:2 * ps * (j + 1)] = cr[j, sq]
            fini_ref[sq, :, 2 * ps * j:2 * ps * (j + 1)] = ci[j, sq]
    if colmajor:
        per = ncol * SUBLANES
        for tau in range(S5_CHUNK):
            yt = xy_scr[:, tau].reshape(ch, n).T
            for rh in range(nrh):
                y_ref[rh * S5_CHUNK + tau] = yt[rh * per:(rh + 1) * per]
    else:
        for tp in range(S5_CHUNK // 2):
            pair = [xy_scr[:, 2 * tp + e].reshape(ch, n).T.astype(F32).reshape(nseq, nchunk, SUBLANES, ch)
                    for e in range(2)]
            y_ref[:, :, tp] = jnp.concatenate(pair, axis=2).astype(BF16)


def _s5_call(xs, m, p, q, a16_re, a16_im, h0_re, h0_im, reverse, colmajor):
    ch = xs.shape[-1]
    ngrp = ch // S5_GROUP
    ns = ngrp * S5_STATE
    if colmajor:
        nrow = xs.shape[0]
        g, nseq = 1, 1
        ncol = S5_T_BLK // nrow
        nt = xs.shape[1] // (ncol * SUBLANES)
        n = S5_T_BLK // S5_CHUNK * SUBLANES
        blk = (lambda ti: nt - 1 - ti) if reverse else (lambda ti: ti)
        arr = xs
        main = pl.BlockSpec((nrow, ncol * SUBLANES, ch), lambda gi, ti: (0, blk(ti), 0))
    else:
        g, t8, _ = xs.shape
        t = t8 // SUBLANES
        tt = min(S5_T_BLK, t)
        nt = t // tt
        nseq = min(g, S5_T_BLK // tt)
        nchunk = tt // S5_CHUNK
        n = nseq * nchunk * SUBLANES
        blk = (lambda ti: nt - 1 - ti) if reverse else (lambda ti: ti)
        arr = xs.reshape(g, t // S5_CHUNK, S5_CHUNK // 2, 2 * SUBLANES, ch)
        main = pl.BlockSpec((nseq, nchunk, S5_CHUNK // 2, 2 * SUBLANES, ch), lambda gi, ti: (gi, blk(ti), 0, 0, 0))
    st = pl.BlockSpec((nseq, SUBLANES, ns), lambda gi, ti: (gi, 0, 0))
    y, fr, fi = pl.pallas_call(
        functools.partial(_s5_kernel, reverse=reverse, colmajor=colmajor),
        grid=(g // nseq, nt),
        in_specs=[main, _const_spec(m.shape), _const_spec(p.shape), _const_spec(q.shape), _const_spec(a16_re.shape),
                  _const_spec(a16_im.shape), st, st],
        out_specs=[main, st, st],
        out_shape=[jax.ShapeDtypeStruct(arr.shape, BF16), jax.ShapeDtypeStruct((g, SUBLANES, ns), F32),
                   jax.ShapeDtypeStruct((g, SUBLANES, ns), F32)],
        scratch_shapes=[pltpu.VMEM((ngrp, S5_CHUNK, S5_GROUP, n), BF16),
                        pltpu.VMEM((S5_PAIR_UNROLL, n, 4 * S5_STATE), F32),
                        pltpu.VMEM((S5_PAIR_UNROLL, n, 4 * S5_STATE), F32),
                        pltpu.VMEM((ngrp // 2, nseq, SUBLANES, 2 * S5_STATE), F32),
                        pltpu.VMEM((ngrp // 2, nseq, SUBLANES, 2 * S5_STATE), F32)],
        compiler_params=_params(("arbitrary", "arbitrary")),
        name="s5_bwd" if reverse else "s5_fwd",
    )(arr, m, p, q, a16_re, a16_im, h0_re, h0_im)
    return y.reshape(xs.shape), fr, fi


def _toeplitz_kernel(lag_ref, m_ref, *, reverse):
    ng, hh, width = lag_ref.shape
    n = S5_CHUNK * hh
    q0 = S5_CHUNK if reverse else S5_CHUNK - 1
    for g in range(ng):
        x = lag_ref[g]
        for t in range(S5_CHUNK):
            start = (q0 - t) * hh
            win = x if start == 0 else pltpu.roll(x, width - start, axis=1)
            m_ref[g, t * hh:(t + 1) * hh, :] = win[:, 0:n].astype(BF16)


def _toeplitz_call(lags, reverse):
    ngrp, hh, width = lags.shape
    n = S5_CHUNK * hh
    gb = 8
    return pl.pallas_call(
        functools.partial(_toeplitz_kernel, reverse=reverse),
        grid=(ngrp // gb,),
        in_specs=[pl.BlockSpec((gb, hh, width), lambda i: (i, 0, 0))],
        out_specs=pl.BlockSpec((gb, n, n), lambda i: (i, 0, 0)),
        out_shape=jax.ShapeDtypeStruct((ngrp, n, n), BF16),
        compiler_params=_params(("arbitrary",)),
        name="s5_toeplitz",
    )(lags)


def _s5_chunk_operators(a_re, a_im, log_dt, b_re, b_im, c_re, c_im, reverse):
    hi_p = lax.Precision.HIGHEST
    L = S5_CHUNK
    ngrp, ps, hh = b_re.shape
    dt = jnp.exp(log_dt)[:, None]
    lr, li = a_re * dt, a_im * dt

    def powers(e, xr, xi):
        mag = jnp.exp(e * xr)
        return mag * jnp.cos(e * xi), mag * jnp.sin(e * xi)

    abar_re, abar_im = powers(1.0, lr, li)
    den = a_re * a_re + a_im * a_im
    nr, ni = abar_re - 1.0, abar_im
    f_re = (nr * a_re + ni * a_im) / den
    f_im = (ni * a_re - nr * a_im) / den
    bb_re = f_re[..., None] * b_re - f_im[..., None] * b_im
    bb_im = f_re[..., None] * b_im + f_im[..., None] * b_re

    steps = jnp.arange(L, dtype=F32)
    lag_k = steps if reverse else (L - 1.0 - steps)
    pk_re, pk_im = powers(lag_k[None, :, None], lr[:, None, :], li[:, None, :])
    d_re = pk_re[..., None] * bb_re[:, None] - pk_im[..., None] * bb_im[:, None]
    d_im = pk_re[..., None] * bb_im[:, None] + pk_im[..., None] * bb_re[:, None]
    kk = (jnp.einsum("gip,gkpj->gikj", c_re, d_re, precision=hi_p)
          - jnp.einsum("gip,gkpj->gikj", c_im, d_im, precision=hi_p)).reshape(ngrp, hh, L * hh)
    zeros = jnp.zeros_like(kk)
    lags = jnp.concatenate([zeros, kk] if reverse else [kk, zeros], axis=-1)
    m = _toeplitz_call(lags, reverse)

    lr2 = jnp.concatenate([lr, lr], axis=-1)
    li2 = jnp.concatenate([li, li], axis=-1)
    e_p = steps if reverse else (L - 1.0 - steps)
    pr2, pi2 = powers(e_p[None, None, :], lr2[:, :, None], li2[:, :, None])
    pa = jnp.concatenate([bb_re, bb_im], axis=1)
    pb = jnp.concatenate([-bb_im, bb_re], axis=1)
    p = (pr2[..., None] * pa[:, :, None, :] + pi2[..., None] * pb[:, :, None, :]).reshape(ngrp, 2 * ps, L * hh)

    e_q = (L - steps) if reverse else (steps + 1.0)
    qr2, qi2 = powers(e_q[None, :, None], lr2[:, None, :], li2[:, None, :])
    qa = jnp.concatenate([c_re, -c_im], axis=-1)
    qb = jnp.concatenate([-c_im, -c_re], axis=-1)
    q = (qa[:, None] * qr2[:, :, None, :] + qb[:, None] * qi2[:, :, None, :]).reshape(ngrp, L * hh, 2 * ps)

    a16_re, a16_im = powers(float(L), lr, li)
    return (m, p.astype(BF16), q.astype(BF16), a16_re.reshape(ngrp // 2, 1, 2 * ps),
            a16_im.reshape(ngrp // 2, 1, 2 * ps))


def _mix_kernel(x_ref, ga_ref, hf_ref, hb_ref, yf_ref, yb_ref, xs_ref, mod_ref, gpre_ref, gpost_ref,
                wg_ref, bg_ref, wpl_ref, wps_ref, wo_ref, d_ref, wglu_ref, bglu_ref, o_ref):
    rows, d = x_ref.shape
    tt = rows // SUBLANES
    x3 = x_ref[...].reshape(tt, SUBLANES, d)
    sh1 = mod_ref[:, 0:d][None]
    sc1 = mod_ref[:, d:2 * d][None]
    g1 = mod_ref[:, 2 * d:3 * d][None]
    hn = (_rms(x3) * gpre_ref[...] * (1.0 + sc1) + sh1).reshape(rows, d).astype(BF16)
    zg = _dot(hn, wg_ref[...])

    ha = hf_ref[...].astype(F32) + hb_ref[...].astype(F32)
    ya = jax.nn.gelu(ga_ref[...].astype(F32)) * ha
    pa = _dot(ya.astype(BF16), wpl_ref[...])

    us = xs_ref[...].astype(F32)
    hs = yf_ref[...].astype(F32) + yb_ref[...].astype(F32)
    vs = jax.nn.gelu(hs + d_ref[...] * us)
    zglu = _dot(vs.astype(BF16), wglu_ref[...])
    tg = jnp.tanh(zg + bg_ref[...])
    ys = vs * jax.nn.sigmoid(zglu + bglu_ref[...])
    pb = _dot(ys.astype(BF16), wps_ref[...])

    mm = (pa + tg[:, 0:d] * pa) + (pb + tg[:, d:2 * d] * pb)
    m = _dot(mm.astype(BF16), wo_ref[...])
    mn = (_rms(m) * gpost_ref[...]).reshape(tt, SUBLANES, d)
    o_ref[...] = (x3 + g1 * mn).reshape(rows, d)


def _mix_call(xtm, ga, hf, hb, yf, yb, xs, mod8, g_pre, g_post, w_gate, b_gate, w_pl, w_ps, w_out,
              s5_d, w_glu, b_glu, colmajor):
    g, n, d = xtm.shape
    tt = T_BLK
    rows = tt * SUBLANES
    nt = n // rows
    ch = xs.shape[-1]
    row_spec = lambda c: pl.BlockSpec((None, rows, c), lambda gi, ti: (gi, ti, 0))
    if colmajor:
        assert tt == GRID_W
        s5_spec = pl.BlockSpec((None, rows, ch), lambda gi, ti: (ti, 0, 0))
    else:
        s5_spec = row_spec(ch)
    dl = ga.shape[-1]
    consts = [mod8, g_pre.reshape(1, d), g_post.reshape(1, d), w_gate, 0.5 * b_gate.reshape(1, 2 * d), w_pl, w_ps,
              w_out, s5_d.reshape(1, ch), w_glu, b_glu.reshape(1, ch)]
    return pl.pallas_call(
        _mix_kernel,
        grid=(g, nt),
        in_specs=[row_spec(d), row_spec(dl), row_spec(dl), row_spec(dl), s5_spec, s5_spec, s5_spec]
                 + [_const_spec(a.shape) for a in consts],
        out_specs=row_spec(d),
        out_shape=jax.ShapeDtypeStruct((g, n, d), F32),
        compiler_params=_params(("arbitrary", "arbitrary")),
        name="mix",
    )(xtm, ga, hf, hb, yf, yb, xs, *consts)


FF_CHUNKS = 11


def _ffn_kernel(x_ref, mod_ref, gpre_ref, gpost_ref, w13_ref, w2_ref, o_ref):
    rows, d = x_ref.shape
    tt = rows // SUBLANES
    x3 = x_ref[...].reshape(tt, SUBLANES, d)
    sh2 = mod_ref[:, 3 * d:4 * d][None]
    sc2 = mod_ref[:, 4 * d:5 * d][None]
    g2 = mod_ref[:, 5 * d:6 * d][None]
    hn = (_rms(x3) * gpre_ref[...] * (1.0 + sc2) + sh2).reshape(rows, d).astype(BF16)
    dff = w2_ref.shape[0]
    cw = dff // FF_CHUNKS
    f = jnp.zeros((rows, d), F32)
    for k in range(FF_CHUNKS):
        cs = slice(k * cw, (k + 1) * cw)
        u1 = _dot(hn, w13_ref[:, cs])
        u3 = _dot(hn, w13_ref[:, dff + k * cw:dff + (k + 1) * cw])
        act = (u1 * jax.nn.sigmoid(u1) * u3).astype(BF16)
        f = f + _dot(act, w2_ref[cs, :])
    fn = (_rms(f) * gpost_ref[...]).reshape(tt, SUBLANES, d)
    o_ref[...] = jnp.swapaxes(x3 + g2 * fn, 0, 1)


def _ffn_call(x1, mod8, g_pre, g_post, w13, w2):
    g, n, d = x1.shape
    tt = T_BLK
    rows = tt * SUBLANES
    nt = n // rows
    consts = [mod8, g_pre.reshape(1, d), g_post.reshape(1, d), w13, w2]
    return pl.pallas_call(
        _ffn_kernel,
        grid=(g, nt),
        in_specs=[pl.BlockSpec((None, rows, d), lambda gi, ti: (gi, ti, 0))]
                 + [_const_spec(a.shape) for a in consts],
        out_specs=pl.BlockSpec((SUBLANES, tt, d), lambda gi, ti: (gi, ti, 0)),
        out_shape=jax.ShapeDtypeStruct((g * SUBLANES, nt * tt, d), F32),
        compiler_params=_params(("arbitrary", "arbitrary")),
        name="ffn",
    )(x1, *consts)


def _block_diag_tiles(w, per_tile):
    nb, k, m = w.shape
    w = w.reshape(nb // per_tile, per_tile, k, m)
    eye = jnp.eye(per_tile, dtype=w.dtype)
    t = jnp.einsum("tbkm,bc->tbkcm", w, eye)
    return t.reshape(nb // per_tile, per_tile * k, per_tile * m)


def _layer(x, mod8, p, h0_lru, h0_re, h0_im, colmajor):
    b, t, d = x.shape
    d_lru = p["conv_b"].shape[-1]
    d_s5 = p["s5_d"].shape[-1]
    xtm, xa, ga, xs = _stage1_call(x, mod8, p["g_pre_mix"], p["w_in"], d_lru, d_s5, colmajor)
    hs, fins = [], []
    ys, fr, fi = [], [], []
    hu = None
    for dr in range(2):
        res = _lru_call(xa if dr == 0 else hu, (p["conv_w"], p["conv_b"]) if dr == 0 else None, p["wr_bd"][dr],
                        p["wi_bd"][dr], p["lru_b_r"][dr], p["lru_b_i"][dr], p["lru_lambda"][dr], h0_lru[dr],
                        reverse=bool(dr))
        if dr == 0:
            hu = res[2]
        hs.append(res[0])
        fins.append(res[1])
        y, r_, i_ = _s5_call(xs, *p["s5_ops"][dr], h0_re[dr], h0_im[dr], reverse=bool(dr), colmajor=colmajor)
        ys.append(y)
        fr.append(r_)
        fi.append(i_)
    x1 = _mix_call(xtm, ga, hs[0], hs[1], ys[0], ys[1], xs, mod8, p["g_pre_mix"], p["g_post_mix"], p["w_gate"],
                   p["b_gate"], p["w_proj_lru"], p["w_proj_s5"], p["w_out"], p["s5_d"], p["s5_w_glu"],
                   p["s5_b_glu"], colmajor)
    out = _ffn_call(x1, mod8, p["g_pre_ffn"], p["g_post_ffn"], p["w_ff13"], p["w_ff2"])
    return out, fins, fr, fi


def kernel(x_prompt, x_sample, c, state_lru, state_s5_re, state_s5_im, c_ctx, w_mod, b_mod, g_pre_mix, g_post_mix,
           g_pre_ffn, g_post_ffn, w_in, conv_w, conv_b, lru_w_r, lru_b_r, lru_w_i, lru_b_i, lru_lambda,
           s5_a_re, s5_a_im, s5_log_dt, s5_b_re, s5_b_im, s5_c_re, s5_c_im, s5_d, s5_w_glu, s5_b_glu,
           w_proj_lru, w_proj_s5, w_gate, b_gate, w_out, w_ff_in, w_ff_out):
    depth = w_mod.shape[0]
    n_ctx, _, d = x_prompt.shape
    n_dec = x_sample.shape[0]
    assert n_dec == SUBLANES and n_ctx % SUBLANES == 0
    g_ctx = n_ctx // SUBLANES
    n_dir, n_grp, n_st = s5_a_re.shape[1:]
    d_lru = conv_b.shape[-1]
    heads_per_tile = MXU_TILE // lru_w_r.shape[-1]

    y_prompt, y_sample = x_prompt, x_sample
    lru_list, re_list, im_list = [], [], []
    for l in range(depth):
        cvecs = jnp.concatenate([c, c_ctx[None], jnp.zeros((SUBLANES - 1, d), F32)], axis=0)
        mod = _mod_call(cvecs, w_mod[l], b_mod[l])
        mod_dec = mod[0:SUBLANES]
        mod_ctx = jnp.broadcast_to(mod[SUBLANES:SUBLANES + 1], (SUBLANES, mod.shape[1]))

        p = {
            "g_pre_mix": g_pre_mix[l], "g_post_mix": g_post_mix[l], "g_pre_ffn": g_pre_ffn[l],
            "g_post_ffn": g_post_ffn[l], "w_in": w_in[l].astype(BF16), "conv_w": conv_w[l], "conv_b": conv_b[l],
            "wr_bd": [_block_diag_tiles(lru_w_r[l, dr], heads_per_tile).astype(BF16) for dr in range(n_dir)],
            "wi_bd": [_block_diag_tiles(lru_w_i[l, dr], heads_per_tile).astype(BF16) for dr in range(n_dir)],
            "lru_b_r": lru_b_r[l], "lru_b_i": lru_b_i[l],
            "lru_lambda": lru_lambda[l],
            "s5_ops": [_s5_chunk_operators(s5_a_re[l, dr], s5_a_im[l, dr], s5_log_dt[l, dr], s5_b_re[l, dr],
                                           s5_b_im[l, dr], s5_c_re[l, dr], s5_c_im[l, dr], reverse=bool(dr))
                       for dr in range(n_dir)],
            "s5_d": s5_d[l], "s5_w_glu": s5_w_glu[l].astype(BF16), "s5_b_glu": s5_b_glu[l],
            "w_proj_lru": w_proj_lru[l].astype(BF16), "w_proj_s5": w_proj_s5[l].astype(BF16),
            "w_gate": (0.5 * w_gate[l]).astype(BF16), "b_gate": b_gate[l], "w_out": (0.5 * w_out[l]).astype(BF16),
            "w_ff13": w_ff_in[l].astype(BF16), "w_ff2": w_ff_out[l].astype(BF16),
        }
        ns = n_grp * n_st
        zero_lru = [jnp.zeros((g_ctx, SUBLANES, d_lru), F32)] * n_dir
        zero_s5 = [jnp.zeros((g_ctx, SUBLANES, ns), F32)] * n_dir
        y_prompt, f_lru, f_re, f_im = _layer(y_prompt, mod_ctx, p, zero_lru, zero_s5, zero_s5, colmajor=False)
        lru_list.append(jnp.stack([f.reshape(n_ctx, d_lru) for f in f_lru], axis=1))
        re_list.append(jnp.stack([f.reshape(n_ctx, n_grp, n_st) for f in f_re], axis=1))
        im_list.append(jnp.stack([f.reshape(n_ctx, n_grp, n_st) for f in f_im], axis=1))

        h0_lru = [state_lru[:, l, dr].astype(F32)[None] for dr in range(n_dir)]
        h0_re = [state_s5_re[:, l, dr].reshape(1, n_dec, ns) for dr in range(n_dir)]
        h0_im = [state_s5_im[:, l, dr].reshape(1, n_dec, ns) for dr in range(n_dir)]
        y_sample, _, _, _ = _layer(y_sample, mod_dec, p, h0_lru, h0_re, h0_im, colmajor=True)
    return (y_prompt, y_sample, jnp.stack(lru_list, axis=1), jnp.stack(re_list, axis=1),
            jnp.stack(im_list, axis=1))
```

```python
import functools

import jax
import jax.numpy as jnp
from jax import lax
from jax.experimental import pallas as pl
from jax.experimental.pallas import tpu as pltpu

F32 = jnp.float32
BF16 = jnp.bfloat16

EPS = 1e-6
LRU_C = 8.0
LOG2_E = 1.4426950408889634
GRID_W = 64
SUBLANES = 8
MXU_TILE = 256
S5_GROUP = 16
S5_STATE = 64
VMEM_LIMIT = 56 * 1024 * 1024

T_BLK = 64
LRU_T_BLK = 128
STAGE1_T_BLK = 128


def _const_spec(shape):
    nd = len(shape)
    return pl.BlockSpec(shape, lambda *_: (0,) * nd, pipeline_mode=pl.Buffered(1))


def _params(sem):
    return pltpu.CompilerParams(dimension_semantics=sem, vmem_limit_bytes=VMEM_LIMIT)


def _dot(a, b):
    return jnp.dot(a, b, preferred_element_type=F32)


def _rms(x):
    return x * lax.rsqrt(jnp.mean(x * x, axis=-1, keepdims=True) + EPS)


def _mod_kernel(c_ref, w_ref, b_ref, o_ref):
    c = c_ref[...]
    s = c * jax.nn.sigmoid(c)
    o_ref[...] = jnp.dot(s, w_ref[...], preferred_element_type=F32,
                         precision=lax.Precision.HIGHEST) + b_ref[...]


def _mod_call(cvecs, w_mod, b_mod):
    rows, d = cvecs.shape
    n = w_mod.shape[1]
    nb = 4
    return pl.pallas_call(
        _mod_kernel,
        grid=(nb,),
        in_specs=[pl.BlockSpec((rows, d), lambda j: (0, 0)),
                  pl.BlockSpec((d, n // nb), lambda j: (0, j)),
                  pl.BlockSpec((1, n // nb), lambda j: (0, j))],
        out_specs=pl.BlockSpec((rows, n // nb), lambda j: (0, j)),
        out_shape=jax.ShapeDtypeStruct((rows, n), F32),
        compiler_params=_params(("arbitrary",)),
        name="mod",
    )(cvecs, w_mod, b_mod.reshape(1, n))


def _stage1_kernel(x_ref, mod_ref, g_ref, w_ref, xtm_ref, xa_ref, ga_ref, xs_ref):
    _, tt, d = x_ref.shape
    rows = tt * SUBLANES
    x = jnp.swapaxes(x_ref[...], 0, 1)
    sh1 = mod_ref[:, 0:d][None]
    sc1 = mod_ref[:, d:2 * d][None]
    hn = _rms(x) * g_ref[...] * (1.0 + sc1) + sh1
    xtm_ref[...] = x.reshape(rows, d)
    z = _dot(hn.reshape(rows, d).astype(BF16), w_ref[...])
    dl = xa_ref.shape[-1]
    xa_ref[...] = z[:, 0:dl].astype(BF16)
    ga_ref[...] = z[:, dl:2 * dl].astype(BF16)
    zs = z[:, 2 * dl:]
    xs_ref[...] = zs.astype(BF16).reshape(xs_ref.shape)


def _stage1_call(x, mod8, g_pre, w_in, d_lru, d_s5, colmajor):
    b, t, d = x.shape
    g = b // SUBLANES
    tt = STAGE1_T_BLK
    nt = t // tt
    rows = tt * SUBLANES
    row_spec = lambda c: pl.BlockSpec((None, rows, c), lambda gi, ti: (gi, ti, 0))
    if colmajor:
        assert g == 1 and tt % GRID_W == 0
        per = GRID_W * SUBLANES
        xs_shape = (t // GRID_W, per, d_s5)
        xs_spec = pl.BlockSpec((tt // GRID_W, per, d_s5), lambda gi, ti: (ti, 0, 0))
    else:
        xs_shape = (g, t * SUBLANES, d_s5)
        xs_spec = row_spec(d_s5)
    return pl.pallas_call(
        _stage1_kernel,
        grid=(g, nt),
        in_specs=[pl.BlockSpec((SUBLANES, tt, d), lambda gi, ti: (gi, ti, 0)),
                  _const_spec(mod8.shape), _const_spec((1, d)), _const_spec(w_in.shape)],
        out_specs=[row_spec(d), row_spec(d_lru), row_spec(d_lru), xs_spec],
        out_shape=[jax.ShapeDtypeStruct((g, t * SUBLANES, d), F32),
                   jax.ShapeDtypeStruct((g, t * SUBLANES, d_lru), BF16),
                   jax.ShapeDtypeStruct((g, t * SUBLANES, d_lru), BF16),
                   jax.ShapeDtypeStruct(xs_shape, BF16)],
        compiler_params=_params(("arbitrary", "arbitrary")),
        name="stage1",
    )(x, mod8, g_pre.reshape(1, d), w_in)


def _lru_kernel(*refs, reverse, nt, from_conv):
    if from_conv:
        (xm_ref, xp_ref, xn_ref, cw_ref, cb_ref, wr_ref, wi_ref, br_ref, bi_ref, lam_ref, h0_ref,
         h_ref, fin_ref, hu_out_ref, xe_scr, a_scr, b_scr, carry) = refs
        rows, c = xm_ref.shape
    else:
        (hu_ref, wr_ref, wi_ref, br_ref, bi_ref, lam_ref, h0_ref, h_ref, fin_ref, a_scr, b_scr, carry) = refs
        rows, c = hu_ref.shape
    tt = rows // SUBLANES
    i = pl.program_id(1)
    blk = (nt - 1 - i) if reverse else i

    @pl.when(i == 0)
    def _():
        carry[...] = h0_ref[...]

    if from_conv:
        halo = 2 * SUBLANES
        prev_ok = (blk > 0).astype(F32)
        next_ok = (blk < nt - 1).astype(F32)
        xe_scr[0:halo] = xp_ref[...].astype(F32) * prev_ok
        xe_scr[halo:halo + rows] = xm_ref[...].astype(F32)
        xe_scr[halo + rows:halo + rows + SUBLANES] = xn_ref[0:SUBLANES].astype(F32) * next_ok

    for j in range(c // MXU_TILE):
        cs = slice(j * MXU_TILE, (j + 1) * MXU_TILE)
        cc = (-0.5 * LRU_C * LOG2_E) * jax.nn.softplus(-lam_ref[:, cs])
        if from_conv:
            hu = 0.5 * cb_ref[:, cs]
            for k in range(4):
                hu = hu + (0.5 * cw_ref[k:k + 1, cs]) * xe_scr[k * SUBLANES:k * SUBLANES + rows, cs]
            hub = hu.astype(BF16)
            hu_out_ref[:, cs] = hub
        else:
            hub = hu_ref[:, cs]
            hu = hub.astype(F32)
        tr = jnp.tanh(_dot(hub, wr_ref[j]) + 0.5 * br_ref[:, cs])
        ti = jnp.tanh(_dot(hub, wi_ref[j]) + 0.5 * bi_ref[:, cs])
        a = jnp.exp2(cc * tr + cc)
        y = 1.0 - a * a
        a_scr[:, cs] = a
        b_scr[:, cs] = jnp.where(y > 0.0, y * lax.rsqrt(y), 0.0) * (hu * ti + hu)

    def step(s, h):
        t = (tt - 1 - s) if reverse else s
        sl = pl.ds(pl.multiple_of(t * SUBLANES, SUBLANES), SUBLANES)
        h = a_scr[sl, :] * h + b_scr[sl, :]
        b_scr[sl, :] = h
        return h

    h = lax.fori_loop(0, tt, step, carry[...], unroll=8)
    carry[...] = h
    fin_ref[...] = h
    h_ref[...] = b_scr[...].astype(BF16)


def _lru_call(x, conv, wr_bd, wi_bd, b_r, b_i, lam, h0, reverse):
    g, n, c = x.shape
    tt = LRU_T_BLK
    rows = tt * SUBLANES
    nt = n // rows
    blk = (lambda ti: nt - 1 - ti) if reverse else (lambda ti: ti)
    main = pl.BlockSpec((None, rows, c), lambda gi, ti: (gi, blk(ti), 0))
    state = pl.BlockSpec((None, SUBLANES, c), lambda gi, ti: (gi, 0, 0))
    vec = lambda: _const_spec((1, c))
    gate_specs = [_const_spec(wr_bd.shape), _const_spec(wi_bd.shape), vec(), vec(), vec(), state]
    gate_args = (wr_bd, wi_bd, b_r.reshape(1, c), b_i.reshape(1, c), lam.reshape(1, c), h0)
    out_specs = [main, state]
    out_shape = [jax.ShapeDtypeStruct((g, n, c), BF16), jax.ShapeDtypeStruct((g, SUBLANES, c), F32)]
    scratch = [pltpu.VMEM((rows, c), F32), pltpu.VMEM((rows, c), F32), pltpu.VMEM((SUBLANES, c), F32)]
    if conv is not None:
        hb = 2 * SUBLANES
        per = rows // hb
        nhb = n // hb
        prev = pl.BlockSpec((None, hb, c), lambda gi, ti: (gi, jnp.maximum(blk(ti) * per - 1, 0), 0))
        nxt = pl.BlockSpec((None, hb, c), lambda gi, ti: (gi, jnp.minimum((blk(ti) + 1) * per, nhb - 1), 0))
        in_specs = [main, prev, nxt, _const_spec((4, c)), vec()] + gate_specs
        args = (x, x, x, conv[0], conv[1].reshape(1, c)) + gate_args
        out_specs = out_specs + [main]
        out_shape = out_shape + [jax.ShapeDtypeStruct((g, n, c), BF16)]
        scratch = [pltpu.VMEM((rows + 4 * SUBLANES, c), F32)] + scratch
    else:
        in_specs = [main] + gate_specs
        args = (x,) + gate_args
    return pl.pallas_call(
        functools.partial(_lru_kernel, reverse=reverse, nt=nt, from_conv=conv is not None),
        grid=(g, nt),
        in_specs=in_specs,
        out_specs=out_specs,
        out_shape=out_shape,
        scratch_shapes=scratch,
        compiler_params=_params(("arbitrary", "arbitrary")),
        name="lru_bwd" if reverse else "lru_fwd",
    )(*args)


S5_CHUNK = 16
S5_T_BLK = 512
S5_PAIR_UNROLL = 16


def _s5_kernel(xs_ref, m_ref, p_ref, q_ref, ar_ref, ai_ref, h0r_ref, h0i_ref, y_ref, finr_ref, fini_ref,
               xy_scr, st_scr, hin_scr, cr, ci, *, reverse, colmajor):
    ch = xs_ref.shape[-1]
    if colmajor:
        nseq = 1
        nrow, ncol = xs_ref.shape[0], xs_ref.shape[1] // SUBLANES
        nrh = nrow // S5_CHUNK
        nchunk = ncol * nrh
        chunk_row = lambda sq, m: ((m % nrh) * ncol + m // nrh) * SUBLANES
    else:
        nseq, nchunk = xs_ref.shape[0], xs_ref.shape[1]
        chunk_row = lambda sq, m: (sq * nchunk + m) * SUBLANES
    n = nseq * nchunk * SUBLANES
    ngrp = ch // S5_GROUP
    npair = ngrp // 2
    kdim = S5_CHUNK * S5_GROUP
    ps = S5_STATE
    unroll = S5_PAIR_UNROLL
    i = pl.program_id(1)

    @pl.when(i == 0)
    def _():
        for j in range(npair):
            for sq in range(nseq):
                cr[j, sq] = h0r_ref[sq, :, 2 * ps * j:2 * ps * (j + 1)]
                ci[j, sq] = h0i_ref[sq, :, 2 * ps * j:2 * ps * (j + 1)]

    if colmajor:
        for tau in range(S5_CHUNK):
            x = jnp.concatenate([xs_ref[rh * S5_CHUNK + tau] for rh in range(nrh)], axis=0)
            xy_scr[:, tau] = x.T.reshape(ngrp, S5_GROUP, n)
    else:
        for tp in range(S5_CHUNK // 2):
            v = xs_ref[:, :, tp].astype(F32)
            for e in range(2):
                x = v[:, :, e * SUBLANES:(e + 1) * SUBLANES, :].reshape(n, ch).astype(BF16)
                xy_scr[:, 2 * tp + e] = x.T.reshape(ngrp, S5_GROUP, n)

    def pairs(jj, carry_):
        base = jj * unroll
        rs = [_dot(jnp.concatenate([m_ref[2 * base + e], p_ref[2 * base + e]], axis=0),
                   xy_scr[2 * base + e].reshape(kdim, n)) for e in range(2 * unroll)]
        for u in range(unroll):
            r0, r1 = rs[2 * u], rs[2 * u + 1]
            st = jnp.concatenate([r0[kdim:kdim + ps], r1[kdim:kdim + ps], r0[kdim + ps:], r1[kdim + ps:]], axis=0)
            st_scr[u] = st.astype(BF16).T.astype(F32)
        ars = [jnp.broadcast_to(ar_ref[base + u], (SUBLANES, 2 * ps)) for u in range(unroll)]
        ais = [jnp.broadcast_to(ai_ref[base + u], (SUBLANES, 2 * ps)) for u in range(unroll)]
        state = [[(cr[base + u, sq], ci[base + u, sq]) for sq in range(nseq)] for u in range(unroll)]
        for s in range(nchunk):
            m = (nchunk - 1 - s) if reverse else s
            for u in range(unroll):
                for sq in range(nseq):
                    row = chunk_row(sq, m)
                    rows = slice(row, row + SUBLANES)
                    hr, hi = state[u][sq]
                    hin_scr[u, rows, 0:2 * ps] = hr
                    hin_scr[u, rows, 2 * ps:4 * ps] = hi
                    sr = st_scr[u, rows, 0:2 * ps]
                    si = st_scr[u, rows, 2 * ps:4 * ps]
                    state[u][sq] = (ars[u] * hr - ais[u] * hi + sr, ars[u] * hi + ais[u] * hr + si)
        for u in range(unroll):
            for sq in range(nseq):
                cr[base + u, sq], ci[base + u, sq] = state[u][sq]
        hts = [hin_scr[u].astype(BF16).T for u in range(unroll)]
        for u in range(unroll):
            ht = hts[u]
            for e in range(2):
                het = jnp.concatenate([ht[e * ps:(e + 1) * ps], ht[(2 + e) * ps:(3 + e) * ps]], axis=0)
                y = rs[2 * u + e][0:kdim] + _dot(q_ref[2 * (base + u) + e], het)
                xy_scr[2 * (base + u) + e] = y.astype(BF16).reshape(S5_CHUNK, S5_GROUP, n)
        return carry_

    lax.fori_loop(0, npair // unroll, pairs, 0)

    for j in range(npair):
        for sq in range(nseq):
            finr_ref[sq, :, 2 * ps * j:2 * ps * (j + 1)] = cr[j, sq]
            fini_ref[sq, :, 2 * ps * j:2 * ps * (j + 1)] = ci[j, sq]
    if colmajor:
        per = ncol * SUBLANES
        for tau in range(S5_CHUNK):
            yt = xy_scr[:, tau].reshape(ch, n).T
            for rh in range(nrh):
                y_ref[rh * S5_CHUNK + tau] = yt[rh * per:(rh + 1) * per]
    else:
        for tp in range(S5_CHUNK // 2):
            pair = [xy_scr[:, 2 * tp + e].reshape(ch, n).T.astype(F32).reshape(nseq, nchunk, SUBLANES, ch)
                    for e in range(2)]
            y_ref[:, :, tp] = jnp.concatenate(pair, axis=2).astype(BF16)


def _s5_call(xs, m, p, q, a16_re, a16_im, h0_re, h0_im, reverse, colmajor):
    ch = xs.shape[-1]
    ngrp = ch // S5_GROUP
    ns = ngrp * S5_STATE
    if colmajor:
        nrow = xs.shape[0]
        g, nseq = 1, 1
        ncol = S5_T_BLK // nrow
        nt = xs.shape[1] // (ncol * SUBLANES)
        n = S5_T_BLK // S5_CHUNK * SUBLANES
        blk = (lambda ti: nt - 1 - ti) if reverse else (lambda ti: ti)
        arr = xs
        main = pl.BlockSpec((nrow, ncol * SUBLANES, ch), lambda gi, ti: (0, blk(ti), 0))
    else:
        g, t8, _ = xs.shape
        t = t8 // SUBLANES
        tt = min(S5_T_BLK, t)
        nt = t // tt
        nseq = min(g, S5_T_BLK // tt)
        nchunk = tt // S5_CHUNK
        n = nseq * nchunk * SUBLANES
        blk = (lambda ti: nt - 1 - ti) if reverse else (lambda ti: ti)
        arr = xs.reshape(g, t // S5_CHUNK, S5_CHUNK // 2, 2 * SUBLANES, ch)
        main = pl.BlockSpec((nseq, nchunk, S5_CHUNK // 2, 2 * SUBLANES, ch), lambda gi, ti: (gi, blk(ti), 0, 0, 0))
    st = pl.BlockSpec((nseq, SUBLANES, ns), lambda gi, ti: (gi, 0, 0))
    y, fr, fi = pl.pallas_call(
        functools.partial(_s5_kernel, reverse=reverse, colmajor=colmajor),
        grid=(g // nseq, nt),
        in_specs=[main, _const_spec(m.shape), _const_spec(p.shape), _const_spec(q.shape), _const_spec(a16_re.shape),
                  _const_spec(a16_im.shape), st, st],
        out_specs=[main, st, st],
        out_shape=[jax.ShapeDtypeStruct(arr.shape, BF16), jax.ShapeDtypeStruct((g, SUBLANES, ns), F32),
                   jax.ShapeDtypeStruct((g, SUBLANES, ns), F32)],
        scratch_shapes=[pltpu.VMEM((ngrp, S5_CHUNK, S5_GROUP, n), BF16),
                        pltpu.VMEM((S5_PAIR_UNROLL, n, 4 * S5_STATE), F32),
                        pltpu.VMEM((S5_PAIR_UNROLL, n, 4 * S5_STATE), F32),
                        pltpu.VMEM((ngrp // 2, nseq, SUBLANES, 2 * S5_STATE), F32),
                        pltpu.VMEM((ngrp // 2, nseq, SUBLANES, 2 * S5_STATE), F32)],
        compiler_params=_params(("arbitrary", "arbitrary")),
        name="s5_bwd" if reverse else "s5_fwd",
    )(arr, m, p, q, a16_re, a16_im, h0_re, h0_im)
    return y.reshape(xs.shape), fr, fi


def _toeplitz_kernel(lag_ref, m_ref, *, reverse):
    ng, hh, width = lag_ref.shape
    n = S5_CHUNK * hh
    q0 = S5_CHUNK if reverse else S5_CHUNK - 1
    for g in range(ng):
        x = lag_ref[g]
        for t in range(S5_CHUNK):
            start = (q0 - t) * hh
            win = x if start == 0 else pltpu.roll(x, width - start, axis=1)
            m_ref[g, t * hh:(t + 1) * hh, :] = win[:, 0:n].astype(BF16)


def _toeplitz_call(lags, reverse):
    ngrp, hh, width = lags.shape
    n = S5_CHUNK * hh
    gb = 8
    return pl.pallas_call(
        functools.partial(_toeplitz_kernel, reverse=reverse),
        grid=(ngrp // gb,),
        in_specs=[pl.BlockSpec((gb, hh, width), lambda i: (i, 0, 0))],
        out_specs=pl.BlockSpec((gb, n, n), lambda i: (i, 0, 0)),
        out_shape=jax.ShapeDtypeStruct((ngrp, n, n), BF16),
        compiler_params=_params(("arbitrary",)),
        name="s5_toeplitz",
    )(lags)


def _s5_chunk_operators(a_re, a_im, log_dt, b_re, b_im, c_re, c_im):
    hi_p = lax.Precision.HIGHEST
    L = S5_CHUNK
    n_dir, ngrp, ps, hh = b_re.shape
    ng = n_dir * ngrp
    flat = lambda x: x.reshape((ng,) + x.shape[2:])
    a_re, a_im, log_dt, b_re, b_im, c_re, c_im = map(flat, (a_re, a_im, log_dt, b_re, b_im, c_re, c_im))
    rev = (jnp.arange(ng) >= ngrp)[:, None]
    dt = jnp.exp(log_dt)[:, None]
    lr, li = a_re * dt, a_im * dt

    def powers(e, xr, xi):
        mag = jnp.exp(e * xr)
        return mag * jnp.cos(e * xi), mag * jnp.sin(e * xi)

    abar_re, abar_im = powers(1.0, lr, li)
    den = a_re * a_re + a_im * a_im
    nr, ni = abar_re - 1.0, abar_im
    f_re = (nr * a_re + ni * a_im) / den
    f_im = (ni * a_re - nr * a_im) / den
    bb_re = f_re[..., None] * b_re - f_im[..., None] * b_im
    bb_im = f_re[..., None] * b_im + f_im[..., None] * b_re

    steps = jnp.arange(L, dtype=F32)[None, :]
    lag_k = jnp.where(rev, steps, L - 1.0 - steps)
    pk_re, pk_im = powers(lag_k[:, :, None], lr[:, None, :], li[:, None, :])
    d_re = pk_re[..., None] * bb_re[:, None] - pk_im[..., None] * bb_im[:, None]
    d_im = pk_re[..., None] * bb_im[:, None] + pk_im[..., None] * bb_re[:, None]
    kk = (jnp.einsum("gip,gkpj->gikj", c_re, d_re, precision=hi_p)
          - jnp.einsum("gip,gkpj->gikj", c_im, d_im, precision=hi_p)).reshape(ng, hh, L * hh)
    rev3 = rev[:, :, None]
    lags = jnp.concatenate([jnp.where(rev3, 0.0, kk), jnp.where(rev3, kk, 0.0)], axis=-1)
    ms = [_toeplitz_call(lags[d * ngrp:(d + 1) * ngrp], bool(d)) for d in range(n_dir)]

    lr2 = jnp.concatenate([lr, lr], axis=-1)
    li2 = jnp.concatenate([li, li], axis=-1)
    pr2, pi2 = powers(lag_k[:, None, :], lr2[:, :, None], li2[:, :, None])
    pa = jnp.concatenate([bb_re, bb_im], axis=1)
    pb = jnp.concatenate([-bb_im, bb_re], axis=1)
    p = (pr2[..., None] * pa[:, :, None, :] + pi2[..., None] * pb[:, :, None, :]).reshape(ng, 2 * ps, L * hh)

    e_q = jnp.where(rev, L - steps, steps + 1.0)
    qr2, qi2 = powers(e_q[:, :, None], lr2[:, None, :], li2[:, None, :])
    qa = jnp.concatenate([c_re, -c_im], axis=-1)
    qb = jnp.concatenate([-c_im, -c_re], axis=-1)
    q = (qa[:, None] * qr2[:, :, None, :] + qb[:, None] * qi2[:, :, None, :]).reshape(ng, L * hh, 2 * ps)

    a16_re, a16_im = powers(float(L), lr, li)
    p, q = p.astype(BF16), q.astype(BF16)
    a16_re = a16_re.reshape(n_dir, ngrp // 2, 1, 2 * ps)
    a16_im = a16_im.reshape(n_dir, ngrp // 2, 1, 2 * ps)
    return [(ms[d], p[d * ngrp:(d + 1) * ngrp], q[d * ngrp:(d + 1) * ngrp], a16_re[d], a16_im[d])
            for d in range(n_dir)]


def _mix_kernel(x_ref, ga_ref, hf_ref, hb_ref, yf_ref, yb_ref, xs_ref, mod_ref, gpre_ref, gpost_ref,
                wg_ref, bg_ref, wpl_ref, wps_ref, wo_ref, d_ref, wglu_ref, bglu_ref, o_ref):
    rows, d = x_ref.shape
    tt = rows // SUBLANES
    x3 = x_ref[...].reshape(tt, SUBLANES, d)
    sh1 = mod_ref[:, 0:d][None]
    sc1 = mod_ref[:, d:2 * d][None]
    g1 = mod_ref[:, 2 * d:3 * d][None]
    hn = (_rms(x3) * gpre_ref[...] * (1.0 + sc1) + sh1).reshape(rows, d).astype(BF16)
    zg = _dot(hn, wg_ref[...])

    ha = hf_ref[...].astype(F32) + hb_ref[...].astype(F32)
    ya = jax.nn.gelu(ga_ref[...].astype(F32)) * ha
    pa = _dot(ya.astype(BF16), wpl_ref[...])

    us = xs_ref[...].astype(F32)
    hs = yf_ref[...].astype(F32) + yb_ref[...].astype(F32)
    vs = jax.nn.gelu(hs + d_ref[...] * us)
    zglu = _dot(vs.astype(BF16), wglu_ref[...])
    tg = jnp.tanh(zg + bg_ref[...])
    ys = vs * jax.nn.sigmoid(zglu + bglu_ref[...])
    pb = _dot(ys.astype(BF16), wps_ref[...])

    mm = (pa + tg[:, 0:d] * pa) + (pb + tg[:, d:2 * d] * pb)
    m = _dot(mm.astype(BF16), wo_ref[...])
    mn = (_rms(m) * gpost_ref[...]).reshape(tt, SUBLANES, d)
    o_ref[...] = (x3 + g1 * mn).reshape(rows, d)


def _mix_call(xtm, ga, hf, hb, yf, yb, xs, mod8, g_pre, g_post, w_gate, b_gate, w_pl, w_ps, w_out,
              s5_d, w_glu, b_glu, colmajor):
    g, n, d = xtm.shape
    tt = T_BLK
    rows = tt * SUBLANES
    nt = n // rows
    ch = xs.shape[-1]
    row_spec = lambda c: pl.BlockSpec((None, rows, c), lambda gi, ti: (gi, ti, 0))
    if colmajor:
        assert tt == GRID_W
        s5_spec = pl.BlockSpec((None, rows, ch), lambda gi, ti: (ti, 0, 0))
    else:
        s5_spec = row_spec(ch)
    dl = ga.shape[-1]
    consts = [mod8, g_pre.reshape(1, d), g_post.reshape(1, d), w_gate, 0.5 * b_gate.reshape(1, 2 * d), w_pl, w_ps,
              w_out, s5_d.reshape(1, ch), w_glu, b_glu.reshape(1, ch)]
    return pl.pallas_call(
        _mix_kernel,
        grid=(g, nt),
        in_specs=[row_spec(d), row_spec(dl), row_spec(dl), row_spec(dl), s5_spec, s5_spec, s5_spec]
                 + [_const_spec(a.shape) for a in consts],
        out_specs=row_spec(d),
        out_shape=jax.ShapeDtypeStruct((g, n, d), F32),
        compiler_params=_params(("arbitrary", "arbitrary")),
        name="mix",
    )(xtm, ga, hf, hb, yf, yb, xs, *consts)


FF_CHUNKS = 11


def _ffn_kernel(x_ref, mod_ref, gpre_ref, gpost_ref, w13_ref, w2_ref, o_ref):
    rows, d = x_ref.shape
    tt = rows // SUBLANES
    x3 = x_ref[...].reshape(tt, SUBLANES, d)
    sh2 = mod_ref[:, 3 * d:4 * d][None]
    sc2 = mod_ref[:, 4 * d:5 * d][None]
    g2 = mod_ref[:, 5 * d:6 * d][None]
    hn = (_rms(x3) * gpre_ref[...] * (1.0 + sc2) + sh2).reshape(rows, d).astype(BF16)
    dff = w2_ref.shape[0]
    cw = dff // FF_CHUNKS
    f = jnp.zeros((rows, d), F32)
    for k in range(FF_CHUNKS):
        cs = slice(k * cw, (k + 1) * cw)
        u1 = _dot(hn, w13_ref[:, cs])
        u3 = _dot(hn, w13_ref[:, dff + k * cw:dff + (k + 1) * cw])
        act = (u1 * jax.nn.sigmoid(u1) * u3).astype(BF16)
        f = f + _dot(act, w2_ref[cs, :])
    fn = (_rms(f) * gpost_ref[...]).reshape(tt, SUBLANES, d)
    o_ref[...] = jnp.swapaxes(x3 + g2 * fn, 0, 1)


def _ffn_call(x1, mod8, g_pre, g_post, w13, w2):
    g, n, d = x1.shape
    tt = T_BLK
    rows = tt * SUBLANES
    nt = n // rows
    consts = [mod8, g_pre.reshape(1, d), g_post.reshape(1, d), w13, w2]
    return pl.pallas_call(
        _ffn_kernel,
        grid=(g, nt),
        in_specs=[pl.BlockSpec((None, rows, d), lambda gi, ti: (gi, ti, 0))]
                 + [_const_spec(a.shape) for a in consts],
        out_specs=pl.BlockSpec((SUBLANES, tt, d), lambda gi, ti: (gi, ti, 0)),
        out_shape=jax.ShapeDtypeStruct((g * SUBLANES, nt * tt, d), F32),
        compiler_params=_params(("arbitrary", "arbitrary")),
        name="ffn",
    )(x1, *consts)


def _block_diag_tiles(w, per_tile):
    nb, k, m = w.shape
    w = w.reshape(nb // per_tile, per_tile, k, m)
    eye = jnp.eye(per_tile, dtype=w.dtype)
    t = jnp.einsum("tbkm,bc->tbkcm", w, eye)
    return t.reshape(nb // per_tile, per_tile * k, per_tile * m)


def _layer(x, mod8, p, h0_lru, h0_re, h0_im, colmajor):
    b, t, d = x.shape
    d_lru = p["conv_b"].shape[-1]
    d_s5 = p["s5_d"].shape[-1]
    xtm, xa, ga, xs = _stage1_call(x, mod8, p["g_pre_mix"], p["w_in"], d_lru, d_s5, colmajor)
    hs, fins = [], []
    ys, fr, fi = [], [], []
    hu = None
    for dr in range(2):
        res = _lru_call(xa if dr == 0 else hu, (p["conv_w"], p["conv_b"]) if dr == 0 else None, p["wr_bd"][dr],
                        p["wi_bd"][dr], p["lru_b_r"][dr], p["lru_b_i"][dr], p["lru_lambda"][dr], h0_lru[dr],
                        reverse=bool(dr))
        if dr == 0:
            hu = res[2]
        hs.append(res[0])
        fins.append(res[1])
        y, r_, i_ = _s5_call(xs, *p["s5_ops"][dr], h0_re[dr], h0_im[dr], reverse=bool(dr), colmajor=colmajor)
        ys.append(y)
        fr.append(r_)
        fi.append(i_)
    x1 = _mix_call(xtm, ga, hs[0], hs[1], ys[0], ys[1], xs, mod8, p["g_pre_mix"], p["g_post_mix"], p["w_gate"],
                   p["b_gate"], p["w_proj_lru"], p["w_proj_s5"], p["w_out"], p["s5_d"], p["s5_w_glu"],
                   p["s5_b_glu"], colmajor)
    out = _ffn_call(x1, mod8, p["g_pre_ffn"], p["g_post_ffn"], p["w_ff13"], p["w_ff2"])
    return out, fins, fr, fi


def kernel(x_prompt, x_sample, c, state_lru, state_s5_re, state_s5_im, c_ctx, w_mod, b_mod, g_pre_mix, g_post_mix,
           g_pre_ffn, g_post_ffn, w_in, conv_w, conv_b, lru_w_r, lru_b_r, lru_w_i, lru_b_i, lru_lambda,
           s5_a_re, s5_a_im, s5_log_dt, s5_b_re, s5_b_im, s5_c_re, s5_c_im, s5_d, s5_w_glu, s5_b_glu,
           w_proj_lru, w_proj_s5, w_gate, b_gate, w_out, w_ff_in, w_ff_out):
    depth = w_mod.shape[0]
    n_ctx, _, d = x_prompt.shape
    n_dec = x_sample.shape[0]
    assert n_dec == SUBLANES and n_ctx % SUBLANES == 0
    g_ctx = n_ctx // SUBLANES
    n_dir, n_grp, n_st = s5_a_re.shape[1:]
    d_lru = conv_b.shape[-1]
    heads_per_tile = MXU_TILE // lru_w_r.shape[-1]

    y_prompt, y_sample = x_prompt, x_sample
    lru_list, re_list, im_list = [], [], []
    for l in range(depth):
        cvecs = jnp.concatenate([c, c_ctx[None], jnp.zeros((SUBLANES - 1, d), F32)], axis=0)
        mod = _mod_call(cvecs, w_mod[l], b_mod[l])
        mod_dec = mod[0:SUBLANES]
        mod_ctx = jnp.broadcast_to(mod[SUBLANES:SUBLANES + 1], (SUBLANES, mod.shape[1]))

        p = {
            "g_pre_mix": g_pre_mix[l], "g_post_mix": g_post_mix[l], "g_pre_ffn": g_pre_ffn[l],
            "g_post_ffn": g_post_ffn[l], "w_in": w_in[l].astype(BF16), "conv_w": conv_w[l], "conv_b": conv_b[l],
            "wr_bd": [_block_diag_tiles(lru_w_r[l, dr], heads_per_tile).astype(BF16) for dr in range(n_dir)],
            "wi_bd": [_block_diag_tiles(lru_w_i[l, dr], heads_per_tile).astype(BF16) for dr in range(n_dir)],
            "lru_b_r": lru_b_r[l], "lru_b_i": lru_b_i[l],
            "lru_lambda": lru_lambda[l],
            "s5_ops": _s5_chunk_operators(s5_a_re[l], s5_a_im[l], s5_log_dt[l], s5_b_re[l], s5_b_im[l],
                                          s5_c_re[l], s5_c_im[l]),
            "s5_d": s5_d[l], "s5_w_glu": s5_w_glu[l].astype(BF16), "s5_b_glu": s5_b_glu[l],
            "w_proj_lru": w_proj_lru[l].astype(BF16), "w_proj_s5": w_proj_s5[l].astype(BF16),
            "w_gate": (0.5 * w_gate[l]).astype(BF16), "b_gate": b_gate[l], "w_out": (0.5 * w_out[l]).astype(BF16),
            "w_ff13": w_ff_in[l].astype(BF16), "w_ff2": w_ff_out[l].astype(BF16),
        }
        ns = n_grp * n_st
        zero_lru = [jnp.zeros((g_ctx, SUBLANES, d_lru), F32)] * n_dir
        zero_s5 = [jnp.zeros((g_ctx, SUBLANES, ns), F32)] * n_dir
        y_prompt, f_lru, f_re, f_im = _layer(y_prompt, mod_ctx, p, zero_lru, zero_s5, zero_s5, colmajor=False)
        lru_list.append(jnp.stack([f.reshape(n_ctx, d_lru) for f in f_lru], axis=1))
        re_list.append(jnp.stack([f.reshape(n_ctx, n_grp, n_st) for f in f_re], axis=1))
        im_list.append(jnp.stack([f.reshape(n_ctx, n_grp, n_st) for f in f_im], axis=1))

        h0_lru = [state_lru[:, l, dr].astype(F32)[None] for dr in range(n_dir)]
        h0_re = [state_s5_re[:, l, dr].reshape(1, n_dec, ns) for dr in range(n_dir)]
        h0_im = [state_s5_im[:, l, dr].reshape(1, n_dec, ns) for dr in range(n_dir)]
        y_sample, _, _, _ = _layer(y_sample, mod_dec, p, h0_lru, h0_re, h0_im, colmajor=True)
    return (y_prompt, y_sample, jnp.stack(lru_list, axis=1), jnp.stack(re_list, axis=1),
            jnp.stack(im_list, axis=1))
```

```python
import functools

import jax
import jax.numpy as jnp
from jax import lax
from jax.experimental import pallas as pl
from jax.experimental.pallas import tpu as pltpu

F32 = jnp.float32
BF16 = jnp.bfloat16

EPS = 1e-6
LRU_C = 8.0
LOG2_E = 1.4426950408889634
GRID_W = 64
SUBLANES = 8
MXU_TILE = 256
S5_GROUP = 16
S5_STATE = 64
VMEM_LIMIT = 56 * 1024 * 1024

T_BLK = 64
LRU_T_BLK = 128
STAGE1_T_BLK = 128


def _const_spec(shape):
    nd = len(shape)
    return pl.BlockSpec(shape, lambda *_: (0,) * nd, pipeline_mode=pl.Buffered(1))


def _params(sem):
    return pltpu.CompilerParams(dimension_semantics=sem, vmem_limit_bytes=VMEM_LIMIT)


def _dot(a, b):
    return jnp.dot(a, b, preferred_element_type=F32)


def _rms(x):
    return x * lax.rsqrt(jnp.mean(x * x, axis=-1, keepdims=True) + EPS)


def _mod_kernel(c_ref, w_ref, b_ref, o_ref):
    c = c_ref[...]
    s = c * jax.nn.sigmoid(c)
    o_ref[...] = jnp.dot(s, w_ref[...], preferred_element_type=F32,
                         precision=lax.Precision.HIGHEST) + b_ref[...]


def _mod_call(cvecs, w_mod, b_mod):
    rows, d = cvecs.shape
    n = w_mod.shape[1]
    nb = 4
    return pl.pallas_call(
        _mod_kernel,
        grid=(nb,),
        in_specs=[pl.BlockSpec((rows, d), lambda j: (0, 0)),
                  pl.BlockSpec((d, n // nb), lambda j: (0, j)),
                  pl.BlockSpec((1, n // nb), lambda j: (0, j))],
        out_specs=pl.BlockSpec((rows, n // nb), lambda j: (0, j)),
        out_shape=jax.ShapeDtypeStruct((rows, n), F32),
        compiler_params=_params(("arbitrary",)),
        name="mod",
    )(cvecs, w_mod, b_mod.reshape(1, n))


def _stage1_kernel(x_ref, mod_ref, g_ref, w_ref, xtm_ref, xa_ref, ga_ref, xs_ref):
    _, tt, d = x_ref.shape
    rows = tt * SUBLANES
    x = jnp.swapaxes(x_ref[...], 0, 1)
    sh1 = mod_ref[:, 0:d][None]
    sc1 = mod_ref[:, d:2 * d][None]
    hn = _rms(x) * g_ref[...] * (1.0 + sc1) + sh1
    xtm_ref[...] = x.reshape(rows, d)
    z = _dot(hn.reshape(rows, d).astype(BF16), w_ref[...])
    dl = xa_ref.shape[-1]
    xa_ref[...] = z[:, 0:dl].astype(BF16)
    ga_ref[...] = z[:, dl:2 * dl].astype(BF16)
    zs = z[:, 2 * dl:]
    xs_ref[...] = zs.astype(BF16).reshape(xs_ref.shape)


def _stage1_call(x, mod8, g_pre, w_in, d_lru, d_s5, colmajor):
    b, t, d = x.shape
    g = b // SUBLANES
    tt = STAGE1_T_BLK
    nt = t // tt
    rows = tt * SUBLANES
    row_spec = lambda c: pl.BlockSpec((None, rows, c), lambda gi, ti: (gi, ti, 0))
    if colmajor:
        assert g == 1 and tt % GRID_W == 0
        per = GRID_W * SUBLANES
        xs_shape = (t // GRID_W, per, d_s5)
        xs_spec = pl.BlockSpec((tt // GRID_W, per, d_s5), lambda gi, ti: (ti, 0, 0))
    else:
        xs_shape = (g, t * SUBLANES, d_s5)
        xs_spec = row_spec(d_s5)
    return pl.pallas_call(
        _stage1_kernel,
        grid=(g, nt),
        in_specs=[pl.BlockSpec((SUBLANES, tt, d), lambda gi, ti: (gi, ti, 0)),
                  _const_spec(mod8.shape), _const_spec((1, d)), _const_spec(w_in.shape)],
        out_specs=[row_spec(d), row_spec(d_lru), row_spec(d_lru), xs_spec],
        out_shape=[jax.ShapeDtypeStruct((g, t * SUBLANES, d), F32),
                   jax.ShapeDtypeStruct((g, t * SUBLANES, d_lru), BF16),
                   jax.ShapeDtypeStruct((g, t * SUBLANES, d_lru), BF16),
                   jax.ShapeDtypeStruct(xs_shape, BF16)],
        compiler_params=_params(("arbitrary", "arbitrary")),
        name="stage1",
    )(x, mod8, g_pre.reshape(1, d), w_in)


def _lru_kernel(*refs, reverse, nt, from_conv):
    if from_conv:
        (xm_ref, xp_ref, xn_ref, cw_ref, cb_ref, wr_ref, wi_ref, br_ref, bi_ref, lam_ref, h0_ref,
         h_ref, fin_ref, hu_out_ref, xe_scr, a_scr, b_scr, carry) = refs
        rows, c = xm_ref.shape
    else:
        (hu_ref, wr_ref, wi_ref, br_ref, bi_ref, lam_ref, h0_ref, h_ref, fin_ref, a_scr, b_scr, carry) = refs
        rows, c = hu_ref.shape
    tt = rows // SUBLANES
    i = pl.program_id(1)
    blk = (nt - 1 - i) if reverse else i

    @pl.when(i == 0)
    def _():
        carry[...] = h0_ref[...]

    if from_conv:
        halo = 2 * SUBLANES
        prev_ok = (blk > 0).astype(F32)
        next_ok = (blk < nt - 1).astype(F32)
        xe_scr[0:halo] = xp_ref[...].astype(F32) * prev_ok
        xe_scr[halo:halo + rows] = xm_ref[...].astype(F32)
        xe_scr[halo + rows:halo + rows + SUBLANES] = xn_ref[0:SUBLANES].astype(F32) * next_ok

    for j in range(c // MXU_TILE):
        cs = slice(j * MXU_TILE, (j + 1) * MXU_TILE)
        cc = (-0.5 * LRU_C * LOG2_E) * jax.nn.softplus(-lam_ref[:, cs])
        if from_conv:
            hu = 0.5 * cb_ref[:, cs]
            for k in range(4):
                hu = hu + (0.5 * cw_ref[k:k + 1, cs]) * xe_scr[k * SUBLANES:k * SUBLANES + rows, cs]
            hub = hu.astype(BF16)
            hu_out_ref[:, cs] = hub
        else:
            hub = hu_ref[:, cs]
            hu = hub.astype(F32)
        tr = jnp.tanh(_dot(hub, wr_ref[j]) + 0.5 * br_ref[:, cs])
        ti = jnp.tanh(_dot(hub, wi_ref[j]) + 0.5 * bi_ref[:, cs])
        a = jnp.exp2(cc * tr + cc)
        y = 1.0 - a * a
        a_scr[:, cs] = a
        b_scr[:, cs] = jnp.where(y > 0.0, y * lax.rsqrt(y), 0.0) * (hu * ti + hu)

    def step(s, h):
        t = (tt - 1 - s) if reverse else s
        sl = pl.ds(pl.multiple_of(t * SUBLANES, SUBLANES), SUBLANES)
        h = a_scr[sl, :] * h + b_scr[sl, :]
        b_scr[sl, :] = h
        return h

    h = lax.fori_loop(0, tt, step, carry[...], unroll=8)
    carry[...] = h
    fin_ref[...] = h
    h_ref[...] = b_scr[...].astype(BF16)


def _lru_call(x, conv, wr_bd, wi_bd, b_r, b_i, lam, h0, reverse):
    g, n, c = x.shape
    tt = LRU_T_BLK
    rows = tt * SUBLANES
    nt = n // rows
    blk = (lambda ti: nt - 1 - ti) if reverse else (lambda ti: ti)
    main = pl.BlockSpec((None, rows, c), lambda gi, ti: (gi, blk(ti), 0))
    state = pl.BlockSpec((None, SUBLANES, c), lambda gi, ti: (gi, 0, 0))
    vec = lambda: _const_spec((1, c))
    gate_specs = [_const_spec(wr_bd.shape), _const_spec(wi_bd.shape), vec(), vec(), vec(), state]
    gate_args = (wr_bd, wi_bd, b_r.reshape(1, c), b_i.reshape(1, c), lam.reshape(1, c), h0)
    out_specs = [main, state]
    out_shape = [jax.ShapeDtypeStruct((g, n, c), BF16), jax.ShapeDtypeStruct((g, SUBLANES, c), F32)]
    scratch = [pltpu.VMEM((rows, c), F32), pltpu.VMEM((rows, c), F32), pltpu.VMEM((SUBLANES, c), F32)]
    if conv is not None:
        hb = 2 * SUBLANES
        per = rows // hb
        nhb = n // hb
        prev = pl.BlockSpec((None, hb, c), lambda gi, ti: (gi, jnp.maximum(blk(ti) * per - 1, 0), 0))
        nxt = pl.BlockSpec((None, hb, c), lambda gi, ti: (gi, jnp.minimum((blk(ti) + 1) * per, nhb - 1), 0))
        in_specs = [main, prev, nxt, _const_spec((4, c)), vec()] + gate_specs
        args = (x, x, x, conv[0], conv[1].reshape(1, c)) + gate_args
        out_specs = out_specs + [main]
        out_shape = out_shape + [jax.ShapeDtypeStruct((g, n, c), BF16)]
        scratch = [pltpu.VMEM((rows + 4 * SUBLANES, c), F32)] + scratch
    else:
        in_specs = [main] + gate_specs
        args = (x,) + gate_args
    return pl.pallas_call(
        functools.partial(_lru_kernel, reverse=reverse, nt=nt, from_conv=conv is not None),
        grid=(g, nt),
        in_specs=in_specs,
        out_specs=out_specs,
        out_shape=out_shape,
        scratch_shapes=scratch,
        compiler_params=_params(("arbitrary", "arbitrary")),
        name="lru_bwd" if reverse else "lru_fwd",
    )(*args)


S5_CHUNK = 16
S5_T_BLK = 512
S5_PAIR_UNROLL = 16


def _s5_kernel(xs_ref, m_ref, p_ref, q_ref, ar_ref, ai_ref, h0r_ref, h0i_ref, y_ref, finr_ref, fini_ref,
               xy_scr, st_scr, hin_scr, cr, ci, *, reverse, colmajor):
    ch = xs_ref.shape[-1]
    if colmajor:
        nseq = 1
        nrow, ncol = xs_ref.shape[0], xs_ref.shape[1] // SUBLANES
        nrh = nrow // S5_CHUNK
        nchunk = ncol * nrh
        chunk_row = lambda sq, m: ((m % nrh) * ncol + m // nrh) * SUBLANES
    else:
        nseq, nchunk = xs_ref.shape[0], xs_ref.shape[1]
        chunk_row = lambda sq, m: (sq * nchunk + m) * SUBLANES
    n = nseq * nchunk * SUBLANES
    ngrp = ch // S5_GROUP
    npair = ngrp // 2
    kdim = S5_CHUNK * S5_GROUP
    ps = S5_STATE
    unroll = S5_PAIR_UNROLL
    i = pl.program_id(1)

    @pl.when(i == 0)
    def _():
        for j in range(npair):
            for sq in range(nseq):
                cr[j, sq] = h0r_ref[sq, :, 2 * ps * j:2 * ps * (j + 1)]
                ci[j, sq] = h0i_ref[sq, :, 2 * ps * j:2 * ps * (j + 1)]

    if colmajor:
        for tau in range(S5_CHUNK):
            x = jnp.concatenate([xs_ref[rh * S5_CHUNK + tau] for rh in range(nrh)], axis=0)
            xy_scr[:, tau] = x.T.reshape(ngrp, S5_GROUP, n)
    else:
        for tp in range(S5_CHUNK // 2):
            v = xs_ref[:, :, tp].astype(F32)
            for e in range(2):
                x = v[:, :, e * SUBLANES:(e + 1) * SUBLANES, :].reshape(n, ch).astype(BF16)
                xy_scr[:, 2 * tp + e] = x.T.reshape(ngrp, S5_GROUP, n)

    def pairs(jj, carry_):
        base = jj * unroll
        rs = [_dot(jnp.concatenate([m_ref[2 * base + e], p_ref[2 * base + e]], axis=0),
                   xy_scr[2 * base + e].reshape(kdim, n)) for e in range(2 * unroll)]
        for u in range(unroll):
            r0, r1 = rs[2 * u], rs[2 * u + 1]
            st = jnp.concatenate([r0[kdim:kdim + ps], r1[kdim:kdim + ps], r0[kdim + ps:], r1[kdim + ps:]], axis=0)
            st_scr[u] = st.astype(BF16).T.astype(F32)
        ars = [jnp.broadcast_to(ar_ref[base + u], (SUBLANES, 2 * ps)) for u in range(unroll)]
        ais = [jnp.broadcast_to(ai_ref[base + u], (SUBLANES, 2 * ps)) for u in range(unroll)]
        state = [[(cr[base + u, sq], ci[base + u, sq]) for sq in range(nseq)] for u in range(unroll)]
        for s in range(nchunk):
            m = (nchunk - 1 - s) if reverse else s
            for u in range(unroll):
                for sq in range(nseq):
                    row = chunk_row(sq, m)
                    rows = slice(row, row + SUBLANES)
                    hr, hi = state[u][sq]
                    hin_scr[u, rows, 0:2 * ps] = hr
                    hin_scr[u, rows, 2 * ps:4 * ps] = hi
                    sr = st_scr[u, rows, 0:2 * ps]
                    si = st_scr[u, rows, 2 * ps:4 * ps]
                    state[u][sq] = (ars[u] * hr - ais[u] * hi + sr, ars[u] * hi + ais[u] * hr + si)
        for u in range(unroll):
            for sq in range(nseq):
                cr[base + u, sq], ci[base + u, sq] = state[u][sq]
        hts = [hin_scr[u].astype(BF16).T for u in range(unroll)]
        for u in range(unroll):
            ht = hts[u]
            for e in range(2):
                het = jnp.concatenate([ht[e * ps:(e + 1) * ps], ht[(2 + e) * ps:(3 + e) * ps]], axis=0)
                y = rs[2 * u + e][0:kdim] + _dot(q_ref[2 * (base + u) + e], het)
                xy_scr[2 * (base + u) + e] = y.astype(BF16).reshape(S5_CHUNK, S5_GROUP, n)
        return carry_

    lax.fori_loop(0, npair // unroll, pairs, 0)

    for j in range(npair):
        for sq in range(nseq):
            finr_ref[sq, :, 2 * ps * j:2 * ps * (j + 1)] = cr[j, sq]
            fini_ref[sq, :, 2 * ps * j:2 * ps * (j + 1)] = ci[j, sq]
    if colmajor:
        per = ncol * SUBLANES
        for tau in range(S5_CHUNK):
            yt = xy_scr[:, tau].reshape(ch, n).T
            for rh in range(nrh):
                y_ref[rh * S5_CHUNK + tau] = yt[rh * per:(rh + 1) * per]
    else:
        for tp in range(S5_CHUNK // 2):
            pair = [xy_scr[:, 2 * tp + e].reshape(ch, n).T.astype(F32).reshape(nseq, nchunk, SUBLANES, ch)
                    for e in range(2)]
            y_ref[:, :, tp] = jnp.concatenate(pair, axis=2).astype(BF16)


def _s5_call(xs, m, p, q, a16_re, a16_im, h0_re, h0_im, reverse, colmajor):
    ch = xs.shape[-1]
    ngrp = ch // S5_GROUP
    ns = ngrp * S5_STATE
    if colmajor:
        nrow = xs.shape[0]
        g, nseq = 1, 1
        ncol = S5_T_BLK // nrow
        nt = xs.shape[1] // (ncol * SUBLANES)
        n = S5_T_BLK // S5_CHUNK * SUBLANES
        blk = (lambda ti: nt - 1 - ti) if reverse else (lambda ti: ti)
        arr = xs
        main = pl.BlockSpec((nrow, ncol * SUBLANES, ch), lambda gi, ti: (0, blk(ti), 0))
    else:
        g, t8, _ = xs.shape
        t = t8 // SUBLANES
        tt = min(S5_T_BLK, t)
        nt = t // tt
        nseq = min(g, S5_T_BLK // tt)
        nchunk = tt // S5_CHUNK
        n = nseq * nchunk * SUBLANES
        blk = (lambda ti: nt - 1 - ti) if reverse else (lambda ti: ti)
        arr = xs.reshape(g, t // S5_CHUNK, S5_CHUNK // 2, 2 * SUBLANES, ch)
        main = pl.BlockSpec((nseq, nchunk, S5_CHUNK // 2, 2 * SUBLANES, ch), lambda gi, ti: (gi, blk(ti), 0, 0, 0))
    st = pl.BlockSpec((nseq, SUBLANES, ns), lambda gi, ti: (gi, 0, 0))
    y, fr, fi = pl.pallas_call(
        functools.partial(_s5_kernel, reverse=reverse, colmajor=colmajor),
        grid=(g // nseq, nt),
        in_specs=[main, _const_spec(m.shape), _const_spec(p.shape), _const_spec(q.shape), _const_spec(a16_re.shape),
                  _const_spec(a16_im.shape), st, st],
        out_specs=[main, st, st],
        out_shape=[jax.ShapeDtypeStruct(arr.shape, BF16), jax.ShapeDtypeStruct((g, SUBLANES, ns), F32),
                   jax.ShapeDtypeStruct((g, SUBLANES, ns), F32)],
        scratch_shapes=[pltpu.VMEM((ngrp, S5_CHUNK, S5_GROUP, n), BF16),
                        pltpu.VMEM((S5_PAIR_UNROLL, n, 4 * S5_STATE), F32),
                        pltpu.VMEM((S5_PAIR_UNROLL, n, 4 * S5_STATE), F32),
                        pltpu.VMEM((ngrp // 2, nseq, SUBLANES, 2 * S5_STATE), F32),
                        pltpu.VMEM((ngrp // 2, nseq, SUBLANES, 2 * S5_STATE), F32)],
        compiler_params=_params(("arbitrary", "arbitrary")),
        name="s5_bwd" if reverse else "s5_fwd",
    )(arr, m, p, q, a16_re, a16_im, h0_re, h0_im)
    return y.reshape(xs.shape), fr, fi


def _toeplitz_kernel(lag_ref, m_ref, *, reverse):
    ng, hh, width = lag_ref.shape
    n = S5_CHUNK * hh
    q0 = S5_CHUNK if reverse else S5_CHUNK - 1
    for g in range(ng):
        x = lag_ref[g]
        for t in range(S5_CHUNK):
            start = (q0 - t) * hh
            win = x if start == 0 else pltpu.roll(x, width - start, axis=1)
            m_ref[g, t * hh:(t + 1) * hh, :] = win[:, 0:n].astype(BF16)


def _toeplitz_call(lags, reverse):
    ngrp, hh, width = lags.shape
    n = S5_CHUNK * hh
    gb = 8
    return pl.pallas_call(
        functools.partial(_toeplitz_kernel, reverse=reverse),
        grid=(ngrp // gb,),
        in_specs=[pl.BlockSpec((gb, hh, width), lambda i: (i, 0, 0))],
        out_specs=pl.BlockSpec((gb, n, n), lambda i: (i, 0, 0)),
        out_shape=jax.ShapeDtypeStruct((ngrp, n, n), BF16),
        compiler_params=_params(("arbitrary",)),
        name="s5_toeplitz",
    )(lags)


def _s5_chunk_operators(a_re, a_im, log_dt, b_re, b_im, c_re, c_im):
    hi_p = lax.Precision.HIGHEST
    L = S5_CHUNK
    n_dir, ngrp, ps, hh = b_re.shape
    ng = n_dir * ngrp
    flat = lambda x: x.reshape((ng,) + x.shape[2:])
    a_re, a_im, log_dt, b_re, b_im, c_re, c_im = map(flat, (a_re, a_im, log_dt, b_re, b_im, c_re, c_im))
    rev = (jnp.arange(ng) >= ngrp)[:, None]
    dt = jnp.exp(log_dt)[:, None]
    lr, li = a_re * dt, a_im * dt

    def powers(e, xr, xi):
        mag = jnp.exp(e * xr)
        return mag * jnp.cos(e * xi), mag * jnp.sin(e * xi)

    abar_re, abar_im = powers(1.0, lr, li)
    den = a_re * a_re + a_im * a_im
    nr, ni = abar_re - 1.0, abar_im
    f_re = (nr * a_re + ni * a_im) / den
    f_im = (ni * a_re - nr * a_im) / den
    bb_re = f_re[..., None] * b_re - f_im[..., None] * b_im
    bb_im = f_re[..., None] * b_im + f_im[..., None] * b_re

    steps = jnp.arange(L, dtype=F32)[None, :]
    lag_k = jnp.where(rev, steps, L - 1.0 - steps)
    pk_re, pk_im = powers(lag_k[:, :, None], lr[:, None, :], li[:, None, :])
    d_re = pk_re[..., None] * bb_re[:, None] - pk_im[..., None] * bb_im[:, None]
    d_im = pk_re[..., None] * bb_im[:, None] + pk_im[..., None] * bb_re[:, None]
    kk = (jnp.einsum("gip,gkpj->gikj", c_re, d_re, precision=hi_p)
          - jnp.einsum("gip,gkpj->gikj", c_im, d_im, precision=hi_p)).reshape(ng, hh, L * hh)
    rev3 = rev[:, :, None]
    lags = jnp.concatenate([jnp.where(rev3, 0.0, kk), jnp.where(rev3, kk, 0.0)], axis=-1)
    ms = [_toeplitz_call(lags[d * ngrp:(d + 1) * ngrp], bool(d)) for d in range(n_dir)]

    lr2 = jnp.concatenate([lr, lr], axis=-1)
    li2 = jnp.concatenate([li, li], axis=-1)
    pr2, pi2 = powers(lag_k[:, None, :], lr2[:, :, None], li2[:, :, None])
    pa = jnp.concatenate([bb_re, bb_im], axis=1)
    pb = jnp.concatenate([-bb_im, bb_re], axis=1)
    p = (pr2[..., None] * pa[:, :, None, :] + pi2[..., None] * pb[:, :, None, :]).reshape(ng, 2 * ps, L * hh)

    e_q = jnp.where(rev, L - steps, steps + 1.0)
    qr2, qi2 = powers(e_q[:, :, None], lr2[:, None, :], li2[:, None, :])
    qa = jnp.concatenate([c_re, -c_im], axis=-1)
    qb = jnp.concatenate([-c_im, -c_re], axis=-1)
    q = (qa[:, None] * qr2[:, :, None, :] + qb[:, None] * qi2[:, :, None, :]).reshape(ng, L * hh, 2 * ps)

    a16_re, a16_im = powers(float(L), lr, li)
    p, q = p.astype(BF16), q.astype(BF16)
    a16_re = a16_re.reshape(n_dir, ngrp // 2, 1, 2 * ps)
    a16_im = a16_im.reshape(n_dir, ngrp // 2, 1, 2 * ps)
    return [(ms[d], p[d * ngrp:(d + 1) * ngrp], q[d * ngrp:(d + 1) * ngrp], a16_re[d], a16_im[d])
            for d in range(n_dir)]


def _mix_kernel(x_ref, ga_ref, hf_ref, hb_ref, yf_ref, yb_ref, xs_ref, mod_ref, gpre_ref, gpost_ref,
                wg_ref, bg_ref, wpl_ref, wps_ref, wo_ref, d_ref, wglu_ref, bglu_ref, o_ref):
    rows, d = x_ref.shape
    tt = rows // SUBLANES
    x3 = x_ref[...].reshape(tt, SUBLANES, d)
    sh1 = mod_ref[:, 0:d][None]
    sc1 = mod_ref[:, d:2 * d][None]
    g1 = mod_ref[:, 2 * d:3 * d][None]
    hn = (_rms(x3) * gpre_ref[...] * (1.0 + sc1) + sh1).reshape(rows, d).astype(BF16)
    zga = _dot(hn, wg_ref[:, 0:d])
    zgb = _dot(hn, wg_ref[:, d:2 * d])

    ha = hf_ref[...].astype(F32) + hb_ref[...].astype(F32)
    ya = jax.nn.gelu(ga_ref[...].astype(F32)) * ha
    pa = _dot(ya.astype(BF16), wpl_ref[...])

    us = xs_ref[...].astype(F32)
    hs = yf_ref[...].astype(F32) + yb_ref[...].astype(F32)
    vs = jax.nn.gelu(hs + d_ref[...] * us)
    zglu = _dot(vs.astype(BF16), wglu_ref[...])
    tga = jnp.tanh(zga + bg_ref[:, 0:d])
    tgb = jnp.tanh(zgb + bg_ref[:, d:2 * d])
    ys = vs * jax.nn.sigmoid(zglu + bglu_ref[...])
    pb = _dot(ys.astype(BF16), wps_ref[...])

    mm = (pa + tga * pa) + (pb + tgb * pb)
    m = _dot(mm.astype(BF16), wo_ref[...])
    mn = (_rms(m) * gpost_ref[...]).reshape(tt, SUBLANES, d)
    o_ref[...] = (x3 + g1 * mn).reshape(rows, d)


def _mix_call(xtm, ga, hf, hb, yf, yb, xs, mod8, g_pre, g_post, w_gate, b_gate, w_pl, w_ps, w_out,
              s5_d, w_glu, b_glu, colmajor):
    g, n, d = xtm.shape
    tt = T_BLK
    rows = tt * SUBLANES
    nt = n // rows
    ch = xs.shape[-1]
    row_spec = lambda c: pl.BlockSpec((None, rows, c), lambda gi, ti: (gi, ti, 0))
    if colmajor:
        assert tt == GRID_W
        s5_spec = pl.BlockSpec((None, rows, ch), lambda gi, ti: (ti, 0, 0))
    else:
        s5_spec = row_spec(ch)
    dl = ga.shape[-1]
    consts = [mod8, g_pre.reshape(1, d), g_post.reshape(1, d), w_gate, 0.5 * b_gate.reshape(1, 2 * d), w_pl, w_ps,
              w_out, s5_d.reshape(1, ch), w_glu, b_glu.reshape(1, ch)]
    return pl.pallas_call(
        _mix_kernel,
        grid=(g, nt),
        in_specs=[row_spec(d), row_spec(dl), row_spec(dl), row_spec(dl), s5_spec, s5_spec, s5_spec]
                 + [_const_spec(a.shape) for a in consts],
        out_specs=row_spec(d),
        out_shape=jax.ShapeDtypeStruct((g, n, d), F32),
        compiler_params=_params(("arbitrary", "arbitrary")),
        name="mix",
    )(xtm, ga, hf, hb, yf, yb, xs, *consts)


FF_CHUNKS = 11


def _ffn_kernel(x_ref, mod_ref, gpre_ref, gpost_ref, w13_ref, w2_ref, o_ref):
    rows, d = x_ref.shape
    tt = rows // SUBLANES
    x3 = x_ref[...].reshape(tt, SUBLANES, d)
    sh2 = mod_ref[:, 3 * d:4 * d][None]
    sc2 = mod_ref[:, 4 * d:5 * d][None]
    g2 = mod_ref[:, 5 * d:6 * d][None]
    hn = (_rms(x3) * gpre_ref[...] * (1.0 + sc2) + sh2).reshape(rows, d).astype(BF16)
    dff = w2_ref.shape[0]
    cw = dff // FF_CHUNKS
    f = jnp.zeros((rows, d), F32)
    for k in range(FF_CHUNKS):
        cs = slice(k * cw, (k + 1) * cw)
        u1 = _dot(hn, w13_ref[:, cs])
        u3 = _dot(hn, w13_ref[:, dff + k * cw:dff + (k + 1) * cw])
        act = (u1 * jax.nn.sigmoid(u1) * u3).astype(BF16)
        f = f + _dot(act, w2_ref[cs, :])
    fn = (_rms(f) * gpost_ref[...]).reshape(tt, SUBLANES, d)
    o_ref[...] = jnp.swapaxes(x3 + g2 * fn, 0, 1)


def _ffn_call(x1, mod8, g_pre, g_post, w13, w2):
    g, n, d = x1.shape
    tt = T_BLK
    rows = tt * SUBLANES
    nt = n // rows
    consts = [mod8, g_pre.reshape(1, d), g_post.reshape(1, d), w13, w2]
    return pl.pallas_call(
        _ffn_kernel,
        grid=(g, nt),
        in_specs=[pl.BlockSpec((None, rows, d), lambda gi, ti: (gi, ti, 0))]
                 + [_const_spec(a.shape) for a in consts],
        out_specs=pl.BlockSpec((SUBLANES, tt, d), lambda gi, ti: (gi, ti, 0)),
        out_shape=jax.ShapeDtypeStruct((g * SUBLANES, nt * tt, d), F32),
        compiler_params=_params(("arbitrary", "arbitrary")),
        name="ffn",
    )(x1, *consts)


def _block_diag_tiles(w, per_tile):
    nb, k, m = w.shape
    w = w.reshape(nb // per_tile, per_tile, k, m)
    eye = jnp.eye(per_tile, dtype=w.dtype)
    t = jnp.einsum("tbkm,bc->tbkcm", w, eye)
    return t.reshape(nb // per_tile, per_tile * k, per_tile * m)


def _layer(x, mod8, p, h0_lru, h0_re, h0_im, colmajor):
    b, t, d = x.shape
    d_lru = p["conv_b"].shape[-1]
    d_s5 = p["s5_d"].shape[-1]
    xtm, xa, ga, xs = _stage1_call(x, mod8, p["g_pre_mix"], p["w_in"], d_lru, d_s5, colmajor)
    hs, fins = [], []
    ys, fr, fi = [], [], []
    hu = None
    for dr in range(2):
        res = _lru_call(xa if dr == 0 else hu, (p["conv_w"], p["conv_b"]) if dr == 0 else None, p["wr_bd"][dr],
                        p["wi_bd"][dr], p["lru_b_r"][dr], p["lru_b_i"][dr], p["lru_lambda"][dr], h0_lru[dr],
                        reverse=bool(dr))
        if dr == 0:
            hu = res[2]
        hs.append(res[0])
        fins.append(res[1])
        y, r_, i_ = _s5_call(xs, *p["s5_ops"][dr], h0_re[dr], h0_im[dr], reverse=bool(dr), colmajor=colmajor)
        ys.append(y)
        fr.append(r_)
        fi.append(i_)
    x1 = _mix_call(xtm, ga, hs[0], hs[1], ys[0], ys[1], xs, mod8, p["g_pre_mix"], p["g_post_mix"], p["w_gate"],
                   p["b_gate"], p["w_proj_lru"], p["w_proj_s5"], p["w_out"], p["s5_d"], p["s5_w_glu"],
                   p["s5_b_glu"], colmajor)
    out = _ffn_call(x1, mod8, p["g_pre_ffn"], p["g_post_ffn"], p["w_ff13"], p["w_ff2"])
    return out, fins, fr, fi


def kernel(x_prompt, x_sample, c, state_lru, state_s5_re, state_s5_im, c_ctx, w_mod, b_mod, g_pre_mix, g_post_mix,
           g_pre_ffn, g_post_ffn, w_in, conv_w, conv_b, lru_w_r, lru_b_r, lru_w_i, lru_b_i, lru_lambda,
           s5_a_re, s5_a_im, s5_log_dt, s5_b_re, s5_b_im, s5_c_re, s5_c_im, s5_d, s5_w_glu, s5_b_glu,
           w_proj_lru, w_proj_s5, w_gate, b_gate, w_out, w_ff_in, w_ff_out):
    depth = w_mod.shape[0]
    n_ctx, _, d = x_prompt.shape
    n_dec = x_sample.shape[0]
    assert n_dec == SUBLANES and n_ctx % SUBLANES == 0
    g_ctx = n_ctx // SUBLANES
    n_dir, n_grp, n_st = s5_a_re.shape[1:]
    d_lru = conv_b.shape[-1]
    heads_per_tile = MXU_TILE // lru_w_r.shape[-1]

    y_prompt, y_sample = x_prompt, x_sample
    lru_list, re_list, im_list = [], [], []
    for l in range(depth):
        cvecs = jnp.concatenate([c, c_ctx[None], jnp.zeros((SUBLANES - 1, d), F32)], axis=0)
        mod = _mod_call(cvecs, w_mod[l], b_mod[l])
        mod_dec = mod[0:SUBLANES]
        mod_ctx = jnp.broadcast_to(mod[SUBLANES:SUBLANES + 1], (SUBLANES, mod.shape[1]))

        p = {
            "g_pre_mix": g_pre_mix[l], "g_post_mix": g_post_mix[l], "g_pre_ffn": g_pre_ffn[l],
            "g_post_ffn": g_post_ffn[l], "w_in": w_in[l].astype(BF16), "conv_w": conv_w[l], "conv_b": conv_b[l],
            "wr_bd": [_block_diag_tiles(lru_w_r[l, dr], heads_per_tile).astype(BF16) for dr in range(n_dir)],
            "wi_bd": [_block_diag_tiles(lru_w_i[l, dr], heads_per_tile).astype(BF16) for dr in range(n_dir)],
            "lru_b_r": lru_b_r[l], "lru_b_i": lru_b_i[l],
            "lru_lambda": lru_lambda[l],
            "s5_ops": _s5_chunk_operators(s5_a_re[l], s5_a_im[l], s5_log_dt[l], s5_b_re[l], s5_b_im[l],
                                          s5_c_re[l], s5_c_im[l]),
            "s5_d": s5_d[l], "s5_w_glu": s5_w_glu[l].astype(BF16), "s5_b_glu": s5_b_glu[l],
            "w_proj_lru": w_proj_lru[l].astype(BF16), "w_proj_s5": w_proj_s5[l].astype(BF16),
            "w_gate": (0.5 * w_gate[l]).astype(BF16), "b_gate": b_gate[l], "w_out": (0.5 * w_out[l]).astype(BF16),
            "w_ff13": w_ff_in[l].astype(BF16), "w_ff2": w_ff_out[l].astype(BF16),
        }
        ns = n_grp * n_st
        zero_lru = [jnp.zeros((g_ctx, SUBLANES, d_lru), F32)] * n_dir
        zero_s5 = [jnp.zeros((g_ctx, SUBLANES, ns), F32)] * n_dir
        y_prompt, f_lru, f_re, f_im = _layer(y_prompt, mod_ctx, p, zero_lru, zero_s5, zero_s5, colmajor=False)
        lru_list.append(jnp.stack([f.reshape(n_ctx, d_lru) for f in f_lru], axis=1))
        re_list.append(jnp.stack([f.reshape(n_ctx, n_grp, n_st) for f in f_re], axis=1))
        im_list.append(jnp.stack([f.reshape(n_ctx, n_grp, n_st) for f in f_im], axis=1))

        h0_lru = [state_lru[:, l, dr].astype(F32)[None] for dr in range(n_dir)]
        h0_re = [state_s5_re[:, l, dr].reshape(1, n_dec, ns) for dr in range(n_dir)]
        h0_im = [state_s5_im[:, l, dr].reshape(1, n_dec, ns) for dr in range(n_dir)]
        y_sample, _, _, _ = _layer(y_sample, mod_dec, p, h0_lru, h0_re, h0_im, colmajor=True)
    return (y_prompt, y_sample, jnp.stack(lru_list, axis=1), jnp.stack(re_list, axis=1),
            jnp.stack(im_list, axis=1))
```
